```python
import jax, jax.numpy as jnp
from jax import lax
import numpy as np

D_MODEL = 1024
BATCH = 2
SEQ = 8192
DEPTH = 4

N_EVEN = (DEPTH + 1) // 2
N_ODD = DEPTH // 2

ROPE_THETA = 10000.0
NORM_EPS = 1e-6

MLA_HEADS = 8
MLA_Q_LORA = 384
MLA_KV_LORA = 256
MLA_NOPE = 64
MLA_ROPE = 32
MLA_V = 64
ATTN_Q_BLOCK = 128

MOBA_HEADS = 8
MOBA_HEAD_DIM = 64
MOBA_BLOCK = 256
MOBA_TOPK = 3
MOBA_Q_CHUNK = 32

MOBA_QKV = 3 * MOBA_HEADS * MOBA_HEAD_DIM
SPLIT_POINTS = (MLA_Q_LORA, MLA_Q_LORA + MLA_KV_LORA, MLA_Q_LORA + MLA_KV_LORA + MLA_ROPE)
D_IN_PROJ = MLA_Q_LORA + MLA_KV_LORA + MLA_ROPE + MOBA_QKV
D_MIX = MLA_HEADS * MLA_V + MOBA_HEADS * MOBA_HEAD_DIM

CONV_WIDTH = 31

PEER_HEADS = 8
PEER_N_KEYS = 128
PEER_N_EXPERTS = PEER_N_KEYS * PEER_N_KEYS
PEER_D_KEY = 256
PEER_D_HALF = PEER_D_KEY // 2
PEER_TOPK = 16
PEER_TOKEN_CHUNK = 128

kernel_name = "hybrid_mla_moba_conformer_peer"


def rms_norm(x, g):
    xf = x.astype(jnp.float32)
    y = xf * lax.rsqrt(jnp.mean(xf * xf, axis=-1, keepdims=True) + NORM_EPS)
    return (y * g.astype(jnp.float32)).astype(x.dtype)


def layer_norm(x, g, b):
    xf = x.astype(jnp.float32)
    mu = jnp.mean(xf, axis=-1, keepdims=True)
    var = jnp.mean(jnp.square(xf - mu), axis=-1, keepdims=True)
    y = (xf - mu) * lax.rsqrt(var + NORM_EPS)
    return (y * g.astype(jnp.float32) + b.astype(jnp.float32)).astype(x.dtype)


def rope_angles(positions, dim):
    inv = 1.0 / (ROPE_THETA ** (jnp.arange(0, dim, 2, dtype=jnp.float32) / dim))
    ang = positions.astype(jnp.float32)[..., None] * inv
    return jnp.cos(ang), jnp.sin(ang)


def apply_rope(x, cos, sin):
    d2 = x.shape[-1] // 2
    x1, x2 = x[..., :d2], x[..., d2:]
    c = cos[:, :, None, :].astype(x.dtype)
    s = sin[:, :, None, :].astype(x.dtype)
    return jnp.concatenate([x1 * c - x2 * s, x2 * c + x1 * s], axis=-1)


def causal_attention_swept(q, k, v, scale):
    B, S, H, dk = q.shape
    nq = S // ATTN_Q_BLOCK
    qb = q.reshape(B, nq, ATTN_Q_BLOCK, H, dk).transpose(1, 0, 2, 3, 4)
    kpos = jnp.arange(S)

    def one_block(args):
        i, qi = args
        s = jnp.einsum('bqhd,bkhd->bhqk', qi, k, preferred_element_type=jnp.float32) * scale
        qpos = i * ATTN_Q_BLOCK + jnp.arange(ATTN_Q_BLOCK)
        s = jnp.where(kpos[None, :] <= qpos[:, None], s, -jnp.inf)
        p = jax.nn.softmax(s, axis=-1).astype(v.dtype)
        return jnp.einsum('bhqk,bkhd->bqhd', p, v)

    o = lax.map(one_block, (jnp.arange(nq), qb))
    return o.transpose(1, 0, 2, 3, 4).reshape(B, S, H, v.shape[-1])


def moba_attention(q, k, v):
    B, S, H, d = q.shape
    nb = -(-S // MOBA_BLOCK)
    pad = nb * MOBA_BLOCK - S
    kp = jnp.pad(k, ((0, 0), (0, pad), (0, 0), (0, 0)))
    vp = jnp.pad(v, ((0, 0), (0, pad), (0, 0), (0, 0)))
    k_blocks = kp.reshape(B, nb, MOBA_BLOCK, H, d).transpose(0, 3, 1, 2, 4)
    v_blocks = vp.reshape(B, nb, MOBA_BLOCK, H, d).transpose(0, 3, 1, 2, 4)
    counts = jnp.minimum(MOBA_BLOCK, S - jnp.arange(nb) * MOBA_BLOCK).astype(jnp.float32)
    k_mean = (k_blocks.astype(jnp.float32).sum(axis=3) / counts[None, None, :, None]).astype(k.dtype)
    topk = min(MOBA_TOPK, nb)
    scale = d ** -0.5
    nc = S // MOBA_Q_CHUNK
    qc = q.reshape(B, nc, MOBA_Q_CHUNK, H, d).transpose(1, 0, 2, 3, 4)
    bidx = jnp.arange(B)[:, None, None, None]
    hidx = jnp.arange(H)[None, :, None, None]
    blk_ids = jnp.arange(nb)

    def one_chunk(args):
        c, qi = args
        qpos = c * MOBA_Q_CHUNK + jnp.arange(MOBA_Q_CHUNK)
        own = (c * MOBA_Q_CHUNK) // MOBA_BLOCK
        gate = jnp.einsum('bqhd,bhnd->bhqn', qi, k_mean, preferred_element_type=jnp.float32)
        gate = jnp.where(blk_ids < own, gate, -jnp.inf)
        _, sel = lax.top_k(gate, topk)
        slot_ok = jnp.arange(topk) < own
        kg = k_blocks[bidx, hidx, sel]
        vg = v_blocks[bidx, hidx, sel]
        s_sel = jnp.einsum('bqhd,bhqjkd->bhqjk', qi, kg, preferred_element_type=jnp.float32) * scale
        s_sel = jnp.where(slot_ok[None, None, None, :, None], s_sel, -jnp.inf)
        ko = lax.dynamic_slice_in_dim(k_blocks, own, 1, axis=2)[:, :, 0]
        vo = lax.dynamic_slice_in_dim(v_blocks, own, 1, axis=2)[:, :, 0]
        s_own = jnp.einsum('bqhd,bhkd->bhqk', qi, ko, preferred_element_type=jnp.float32) * scale
        kpos_own = own * MOBA_BLOCK + jnp.arange(MOBA_BLOCK)
        s_own = jnp.where(kpos_own[None, :] <= qpos[:, None], s_own, -jnp.inf)
        s_all = jnp.concatenate([s_sel.reshape(B, H, MOBA_Q_CHUNK, topk * MOBA_BLOCK), s_own], axis=-1)
        p = jax.nn.softmax(s_all, axis=-1).astype(v.dtype)
        p_sel = p[..., :topk * MOBA_BLOCK].reshape(B, H, MOBA_Q_CHUNK, topk, MOBA_BLOCK)
        p_own = p[..., topk * MOBA_BLOCK:]
        return (jnp.einsum('bhqjk,bhqjkd->bqhd', p_sel, vg)
                + jnp.einsum('bhqk,bhkd->bqhd', p_own, vo))

    o = lax.map(one_chunk, (jnp.arange(nc), qc))
    return o.transpose(1, 0, 2, 3, 4).reshape(B, S, H, d)


def hybrid_attention(hn, w_in, q_norm, w_uq, kv_norm, w_ukv, w_out, cos_r, sin_r, cos_m, sin_m):
    B, S, _ = hn.shape
    proj = hn @ w_in
    c_q, c_kv, k_rope, moba_qkv = jnp.split(proj, SPLIT_POINTS, axis=-1)
    q = (rms_norm(c_q, q_norm) @ w_uq).reshape(B, S, MLA_HEADS, MLA_NOPE + MLA_ROPE)
    q_nope, q_pe = q[..., :MLA_NOPE], q[..., MLA_NOPE:]
    q_pe = apply_rope(q_pe, cos_r, sin_r)
    kv = (rms_norm(c_kv, kv_norm) @ w_ukv).reshape(B, S, MLA_HEADS, MLA_NOPE + MLA_V)
    k_nope, v_mla = kv[..., :MLA_NOPE], kv[..., MLA_NOPE:]
    k_pe = apply_rope(k_rope[:, :, None, :], cos_r, sin_r)
    q_full = jnp.concatenate([q_nope, q_pe], axis=-1)
    k_full = jnp.concatenate([k_nope, jnp.broadcast_to(k_pe, (B, S, MLA_HEADS, MLA_ROPE))], axis=-1)
    o_mla = causal_attention_swept(q_full, k_full, v_mla, (MLA_NOPE + MLA_ROPE) ** -0.5)
    mq, mk, mv = jnp.split(moba_qkv, 3, axis=-1)
    mq = apply_rope(mq.reshape(B, S, MOBA_HEADS, MOBA_HEAD_DIM), cos_m, sin_m)
    mk = apply_rope(mk.reshape(B, S, MOBA_HEADS, MOBA_HEAD_DIM), cos_m, sin_m)
    mv = mv.reshape(B, S, MOBA_HEADS, MOBA_HEAD_DIM)
    o_moba = moba_attention(mq, mk, mv)
    o = jnp.concatenate([o_mla.reshape(B, S, MLA_HEADS * MLA_V),
                         o_moba.reshape(B, S, MOBA_HEADS * MOBA_HEAD_DIM)], axis=-1)
    return o @ w_out


def conformer_conv(hn, w_in, b_in, w_dw, b_dw, ln_g, ln_b, w_out, b_out):
    a = hn @ w_in + b_in
    u, gate = jnp.split(a, 2, axis=-1)
    u = u * jax.nn.sigmoid(gate)
    u = lax.conv_general_dilated(u, w_dw[:, None, :], window_strides=(1,),
                                 padding=((CONV_WIDTH - 1, 0),),
                                 dimension_numbers=('NWC', 'WIO', 'NWC'),
                                 feature_group_count=D_MODEL) + b_dw
    u = jax.nn.silu(layer_norm(u, ln_g, ln_b))
    return u @ w_out + b_out


def peer_ffn(hn, w_q, subkeys, u_tab, v_tab):
    B, S, D = hn.shape
    T = B * S
    xt = hn.reshape(T // PEER_TOKEN_CHUNK, PEER_TOKEN_CHUNK, D)

    def one_chunk(xc):
        q = (xc @ w_q).reshape(PEER_TOKEN_CHUNK, PEER_HEADS, 2, PEER_D_HALF)
        s = jnp.einsum('thcd,hcnd->thcn', q, subkeys, preferred_element_type=jnp.float32)
        s_top, i_top = lax.top_k(s, PEER_TOPK)
        cand = s_top[:, :, 0, :, None] + s_top[:, :, 1, None, :]
        cand_idx = i_top[:, :, 0, :, None] * PEER_N_KEYS + i_top[:, :, 1, None, :]
        best, pos = lax.top_k(cand.reshape(PEER_TOKEN_CHUNK, PEER_HEADS, PEER_TOPK * PEER_TOPK), PEER_TOPK)
        idx = jnp.take_along_axis(cand_idx.reshape(PEER_TOKEN_CHUNK, PEER_HEADS, PEER_TOPK * PEER_TOPK), pos, axis=-1)
        g = jax.nn.softmax(best, axis=-1)
        u = u_tab[idx]
        act = jax.nn.gelu(jnp.einsum('td,thkd->thk', xc, u), approximate=False)
        v = v_tab[idx]
        return jnp.einsum('thk,thkd->td', (g * act).astype(v_tab.dtype), v)

    return lax.map(one_chunk, xt).reshape(B, S, D)


def setup_inputs(seed: int = 0) -> dict:
    key = jax.random.key(seed)
    ks = jax.random.split(key, 24)
    f32 = jnp.float32

    def nrm(k, shape, scale):
        return jax.random.normal(k, shape, f32) * scale

    def gain(k, shape):
        return 1.0 + 0.02 * jax.random.normal(k, shape, f32)

    x = jax.random.normal(ks[0], (BATCH, SEQ, D_MODEL), f32)
    offset = jax.random.randint(ks[1], (BATCH, 1), 0, 4096, dtype=jnp.int32)
    positions = offset + jnp.arange(SEQ, dtype=jnp.int32)[None, :]
    return {
        "x": x,
        "positions": positions,
        "norm_mix": gain(ks[2], (DEPTH, D_MODEL)),
        "norm_ffn": gain(ks[3], (DEPTH, D_MODEL)),
        "norm_final": gain(ks[4], (D_MODEL,)),
        "attn_w_in": nrm(ks[5], (N_EVEN, D_MODEL, D_IN_PROJ), D_MODEL ** -0.5),
        "mla_q_norm": gain(ks[6], (N_EVEN, MLA_Q_LORA)),
        "mla_w_uq": nrm(ks[7], (N_EVEN, MLA_Q_LORA, MLA_HEADS * (MLA_NOPE + MLA_ROPE)), MLA_Q_LORA ** -0.5),
        "mla_kv_norm": gain(ks[8], (N_EVEN, MLA_KV_LORA)),
        "mla_w_ukv": nrm(ks[9], (N_EVEN, MLA_KV_LORA, MLA_HEADS * (MLA_NOPE + MLA_V)), MLA_KV_LORA ** -0.5),
        "attn_w_out": nrm(ks[10], (N_EVEN, D_MIX, D_MODEL), 0.5 * D_MIX ** -0.5),
        "conv_w_in": nrm(ks[11], (N_ODD, D_MODEL, 2 * D_MODEL), D_MODEL ** -0.5),
        "conv_b_in": nrm(ks[12], (N_ODD, 2 * D_MODEL), 0.01),
        "conv_w_dw": nrm(ks[13], (N_ODD, CONV_WIDTH, D_MODEL), CONV_WIDTH ** -0.5),
        "conv_b_dw": nrm(ks[14], (N_ODD, D_MODEL), 0.01),
        "conv_ln_g": gain(ks[15], (N_ODD, D_MODEL)),
        "conv_ln_b": nrm(ks[16], (N_ODD, D_MODEL), 0.01),
        "conv_w_out": nrm(ks[17], (N_ODD, D_MODEL, D_MODEL), 0.5 * D_MODEL ** -0.5),
        "conv_b_out": nrm(ks[18], (N_ODD, D_MODEL), 0.01),
        "peer_w_q": nrm(ks[19], (DEPTH, D_MODEL, PEER_HEADS * PEER_D_KEY), D_MODEL ** -0.5),
        "peer_subkeys": nrm(ks[20], (DEPTH, PEER_HEADS, 2, PEER_N_KEYS, PEER_D_HALF), PEER_D_HALF ** -0.5),
        "peer_u": nrm(ks[21], (DEPTH, PEER_N_EXPERTS, D_MODEL), D_MODEL ** -0.5),
        "peer_v": nrm(ks[22], (DEPTH, PEER_N_EXPERTS, D_MODEL), 0.5 * PEER_HEADS ** -0.5),
    }


def reference(x, positions, norm_mix, norm_ffn, norm_final, attn_w_in, mla_q_norm, mla_w_uq,
              mla_kv_norm, mla_w_ukv, attn_w_out, conv_w_in, conv_b_in, conv_w_dw, conv_b_dw,
              conv_ln_g, conv_ln_b, conv_w_out, conv_b_out, peer_w_q, peer_subkeys, peer_u, peer_v):
    cos_r, sin_r = rope_angles(positions, MLA_ROPE)
    cos_m, sin_m = rope_angles(positions, MOBA_HEAD_DIM)
    h = x
    for layer in range(DEPTH):
        hn = rms_norm(h, norm_mix[layer])
        i = layer // 2
        if layer % 2 == 0:
            mix = hybrid_attention(hn, attn_w_in[i], mla_q_norm[i], mla_w_uq[i], mla_kv_norm[i],
                                   mla_w_ukv[i], attn_w_out[i], cos_r, sin_r, cos_m, sin_m)
        else:
            mix = conformer_conv(hn, conv_w_in[i], conv_b_in[i], conv_w_dw[i], conv_b_dw[i],
                                 conv_ln_g[i], conv_ln_b[i], conv_w_out[i], conv_b_out[i])
        h = h + mix
        h = h + peer_ffn(rms_norm(h, norm_ffn[layer]), peer_w_q[layer], peer_subkeys[layer],
                         peer_u[layer], peer_v[layer])
    return rms_norm(h, norm_final)
```

```python
import functools

import jax
import jax.numpy as jnp
import numpy as np
from jax import lax
from jax.experimental import pallas as pl
from jax.experimental.pallas import tpu as pltpu

F32 = jnp.float32
BF16 = jnp.bfloat16

D_MODEL = 1024
ROPE_THETA = 10000.0
NORM_EPS = 1e-6

MLA_HEADS = 8
MLA_Q_LORA = 384
MLA_KV_LORA = 256
MLA_NOPE = 64
MLA_ROPE = 32
MLA_V = 64

MOBA_HEADS = 8
MOBA_HEAD_DIM = 64
MOBA_BLOCK = 256
MOBA_TOPK = 3

CONV_WIDTH = 31

PEER_HEADS = 8
PEER_N_KEYS = 128
PEER_D_HALF = 128
PEER_TOPK = 16

LANES = 128
HEAD_PAIRS = 4
VMEM_LIMIT = 56 * 1024 * 1024
NEG = -1e30

ATTN_TQ = 512
ATTN_TK = 256
PROJ_TM = 256
PEER_ROUTE_TM = 256
PEER_TM = 512
PEER_A_TILE = 8
CONV_TS = 512
CONV_HALO = 32
CONV_ROWS = 32


def _params(*sem):
    return pltpu.CompilerParams(dimension_semantics=sem, vmem_limit_bytes=VMEM_LIMIT)


def _dot(a, b):
    return jnp.dot(a, b, preferred_element_type=F32)


def _dot_nt(a, b):
    return lax.dot_general(a, b, (((1,), (1,)), ((), ())), preferred_element_type=F32)


def _split(a):
    hi = a.astype(BF16)
    lo = (a - hi.astype(F32)).astype(BF16)
    return hi, lo


def _dot3(ah, al, bh, bl):
    return _dot(ah, bh) + (_dot(ah, bl) + _dot(al, bh))


def _dot3_nt(ah, al, bh, bl):
    return _dot_nt(ah, bh) + (_dot_nt(ah, bl) + _dot_nt(al, bh))


def _rms(x, g):
    return x * lax.rsqrt(jnp.mean(x * x, axis=-1, keepdims=True) + NORM_EPS) * g


def _rope_kernel(pos_ref, fr_ref, fm_ref, cr_ref, sr_ref, cm_ref, sm_ref):
    pos = pos_ref[0].astype(F32)
    ang_r = pos * fr_ref[0:1, :]
    ang_m = pos * fm_ref[0:1, :]
    cr_ref[0] = jnp.cos(ang_r) * fr_ref[1:2, :] + fr_ref[2:3, :]
    sr_ref[0] = jnp.sin(ang_r) * fr_ref[1:2, :]
    cm_ref[0] = jnp.cos(ang_m)
    sm_ref[0] = jnp.sin(ang_m)


def _rope_tables(positions):
    B, S = positions.shape
    ts = 512
    inv_r = 1.0 / (ROPE_THETA ** (jnp.arange(0, MLA_ROPE, 2, dtype=F32) / MLA_ROPE))
    inv_m = 1.0 / (ROPE_THETA ** (jnp.arange(0, MOBA_HEAD_DIM, 2, dtype=F32) / MOBA_HEAD_DIM))
    z32 = jnp.zeros((32,), F32)
    z64 = jnp.zeros((64,), F32)
    fr = jnp.stack([jnp.concatenate([z64, inv_r, inv_r, z32]),
                    jnp.concatenate([z64, jnp.ones((32,), F32), z32]),
                    jnp.concatenate([jnp.ones((64,), F32), z32, z32])])
    fr = jnp.concatenate([fr, jnp.zeros((5, LANES), F32)])
    fm = jnp.concatenate([jnp.tile(inv_m, 4)[None, :], jnp.zeros((7, LANES), F32)])
    tab = jax.ShapeDtypeStruct((B, S, LANES), F32)
    row = pl.BlockSpec((1, ts, LANES), lambda b, i: (b, i, 0))
    return pl.pallas_call(
        _rope_kernel,
        grid=(B, S // ts),
        in_specs=[pl.BlockSpec((1, ts, 1), lambda b, i: (b, i, 0)),
                  pl.BlockSpec((8, LANES), lambda b, i: (0, 0)),
                  pl.BlockSpec((8, LANES), lambda b, i: (0, 0))],
        out_specs=[row, row, row, row],
        out_shape=[tab, tab, tab, tab],
        compiler_params=_params("parallel", "parallel"),
        name="rope_tables",
    )(positions.reshape(B, S, 1), fr, fm)


def _attn_prep_kernel(h_ref, g_ref, w1_ref, w2h_ref, w2l_ref, gq_ref, gkv_ref, wq_ref, wqr_ref,
                      wk_ref, wv_ref, cr_ref, sr_ref, cm_ref, sm_ref,
                      qm_ref, km_ref, vm_ref, qb_ref, qf_ref, kf_ref, kb_ref, vb_ref):
    xn = _rms(h_ref[0], g_ref[...])
    xh, xl = _split(xn)
    p1 = _dot(xh, w1_ref[...])
    p2 = _dot3(xh, xl, w2h_ref[...], w2l_ref[...])
    cq, ckv = p1[:, 0:384], p1[:, 384:640]
    kr1, kr2 = p1[:, 640:768], p1[:, 768:896]
    mv = p1[:, 896:1408]
    cr, sr, cm, sm = cr_ref[0], sr_ref[0], cm_ref[0], sm_ref[0]
    nq = _rms(cq, gq_ref[...]).astype(BF16)
    nkv = _rms(ckv, gkv_ref[...]).astype(BF16)
    qa = _dot(nq, wq_ref[...])
    qr = _dot(nq, wqr_ref[...])
    ka = _dot(nkv, wk_ref[...])
    va = _dot(nkv, wv_ref[...])
    kpe = kr1 * cr + kr2 * sr
    mla_scale = (MLA_NOPE + MLA_ROPE) ** -0.5
    for h in range(MLA_HEADS):
        sl = slice(h * LANES, (h + 1) * LANES)
        qm_ref[0, h] = ((qa[:, sl] * cr + qr[:, sl] * sr) * mla_scale).astype(BF16)
        km_ref[0, h] = (ka[:, sl] + kpe).astype(BF16)
        vm_ref[0, h] = va[:, sl].astype(BF16)
    low = lax.broadcasted_iota(jnp.int32, cm.shape, 1) < MOBA_HEAD_DIM
    moba_scale = MOBA_HEAD_DIM ** -0.5
    for p in range(HEAD_PAIRS):
        sl = slice(p * LANES, (p + 1) * LANES)
        q = p2[:, sl] * cm + p2[:, 512 + p * LANES:512 + (p + 1) * LANES] * sm
        k = p2[:, 1024 + p * LANES:1024 + (p + 1) * LANES] * cm + p2[:, 1536 + p * LANES:1536 + (p + 1) * LANES] * sm
        v = mv[:, sl]
        qf_ref[0, :, sl] = q
        kf_ref[0, :, sl] = k
        qb_ref[0, :, sl] = (q * moba_scale).astype(BF16)
        kb_ref[0, 2 * p] = jnp.where(low, k, 0.0).astype(BF16)
        kb_ref[0, 2 * p + 1] = jnp.where(low, 0.0, k).astype(BF16)
        vb_ref[0, 2 * p] = jnp.where(low, v, 0.0).astype(BF16)
        vb_ref[0, 2 * p + 1] = jnp.where(low, 0.0, v).astype(BF16)


def _attn_prep(h, g, w, tabs):
    B, S, D = h.shape
    tm = PROJ_TM
    const = lambda a: pl.BlockSpec(a.shape, lambda b, i: (0,) * a.ndim)
    row = lambda n: pl.BlockSpec((1, tm, n), lambda b, i: (b, i, 0))
    head = pl.BlockSpec((1, 8, tm, LANES), lambda b, i: (b, 0, i, 0))
    head_shape = jax.ShapeDtypeStruct((B, 8, S, LANES), BF16)
    weights = [g, w["w1"], w["w2h"], w["w2l"], w["gq"], w["gkv"], w["wq"], w["wqr"], w["wk"], w["wv"]]
    return pl.pallas_call(
        _attn_prep_kernel,
        grid=(B, S // tm),
        in_specs=[row(D)] + [const(a) for a in weights] + [row(LANES)] * 4,
        out_specs=[head, head, head, row(512), row(512), row(512), head, head],
        out_shape=[head_shape, head_shape, head_shape,
                   jax.ShapeDtypeStruct((B, S, 512), BF16),
                   jax.ShapeDtypeStruct((B, S, 512), F32),
                   jax.ShapeDtypeStruct((B, S, 512), F32),
                   head_shape, head_shape],
        compiler_params=_params("parallel", "parallel"),
        name="attn_prep",
    )(h, *weights, *tabs)


def _moba_gate_kernel(q_ref, k_ref, sel_ref, kmh_ref, kml_ref):
    i = pl.program_id(2)
    tq = q_ref.shape[1]
    S = k_ref.shape[1]
    nb = S // MOBA_BLOCK

    @pl.when(i == 0)
    def _():
        km = jnp.sum(k_ref[0].reshape(nb, MOBA_BLOCK, LANES), axis=1) / float(MOBA_BLOCK)
        low = lax.broadcasted_iota(jnp.int32, km.shape, 1) < MOBA_HEAD_DIM
        for hh, m in enumerate((jnp.where(low, km, 0.0), jnp.where(low, 0.0, km))):
            hi, lo = _split(m)
            kmh_ref[hh] = hi
            kml_ref[hh] = lo

    qh, ql = _split(q_ref[0])
    row = i * tq + lax.broadcasted_iota(jnp.int32, (tq, nb), 0)
    blk = lax.broadcasted_iota(jnp.int32, (tq, nb), 1)
    valid = blk * MOBA_BLOCK < row - (row & (MOBA_BLOCK - 1))
    for hh in range(2):
        gate = _dot3_nt(qh, ql, kmh_ref[hh], kml_ref[hh])
        g = jnp.where(valid, gate, -jnp.inf)
        rest = g
        for _ in range(MOBA_TOPK - 1):
            top = jnp.max(rest, axis=1, keepdims=True)
            rest = jnp.where(rest == top, -jnp.inf, rest)
        thr = jnp.max(rest, axis=1, keepdims=True)
        sel_ref[0, hh] = jnp.where(valid & (g >= thr), 1.0, 0.0)


def _moba_gate(qf, kf):
    B, S, _ = qf.shape
    tq = ATTN_TQ
    nb = S // MOBA_BLOCK
    return pl.pallas_call(
        _moba_gate_kernel,
        grid=(B, HEAD_PAIRS, S // tq),
        in_specs=[pl.BlockSpec((1, tq, LANES), lambda b, p, i: (b, i, p)),
                  pl.BlockSpec((1, S, LANES), lambda b, p, i: (b, 0, p))],
        out_specs=pl.BlockSpec((1, 2, tq, nb), lambda b, p, i: (b, p, i, 0)),
        out_shape=jax.ShapeDtypeStruct((B, 8, S, nb), F32),
        scratch_shapes=[pltpu.VMEM((2, nb, LANES), BF16), pltpu.VMEM((2, nb, LANES), BF16)],
        compiler_params=_params("parallel", "parallel", "arbitrary"),
        name="moba_gate",
    )(qf, kf)


def _flash_kernel(*refs, moba, paired_q):
    if moba:
        q_ref, k_ref, v_ref, sel_ref, o_ref, m_scr, l_scr, acc_scr = refs
    else:
        q_ref, k_ref, v_ref, o_ref, m_scr, l_scr, acc_scr = refs
    i = pl.program_id(2)
    tq, tk = ATTN_TQ, ATTN_TK
    if paired_q:
        qs = (q_ref[0], q_ref[0])
    else:
        qs = (q_ref[0, 0], q_ref[0, 1])
    m_scr[...] = jnp.full(m_scr.shape, NEG, F32)
    l_scr[...] = jnp.zeros(l_scr.shape, F32)
    acc_scr[...] = jnp.zeros(acc_scr.shape, F32)
    low = lax.broadcasted_iota(jnp.int32, (tq, LANES), 1) < 64
    nsel = sel_ref.shape[3] if moba else 0

    def step(j, diag):
        start = pl.multiple_of(j * tk, tk)
        alphas, pvs = [], []
        for hh in range(2):
            k = k_ref[0, hh, pl.ds(start, tk), :]
            v = v_ref[0, hh, pl.ds(start, tk), :]
            s = _dot_nt(qs[hh], k)
            if moba:
                sel = sel_ref[0, hh]
                col = lax.broadcasted_iota(jnp.int32, sel.shape, 1)
                picked = jnp.sum(jnp.where(col == j, sel, 0.0), axis=1, keepdims=True) > 0.5
            if diag:
                qpos = i * tq + lax.broadcasted_iota(jnp.int32, (tq, tk), 0)
                kpos = j * tk + lax.broadcasted_iota(jnp.int32, (tq, tk), 1)
                ok = kpos <= qpos
                if moba:
                    ok = (ok & (kpos >= qpos - (qpos & (MOBA_BLOCK - 1)))) | picked
                s = jnp.where(ok, s, NEG)
            elif moba:
                s = jnp.where(picked, s, NEG)
            m_prev = m_scr[hh]
            m_next = jnp.maximum(m_prev, jnp.max(s, axis=1, keepdims=True))
            alpha = jnp.exp(m_prev - m_next)
            p = jnp.exp(s - jnp.tile(m_next, (1, tk // LANES)))
            l_scr[hh] = alpha * l_scr[hh] + jnp.sum(p, axis=1, keepdims=True)
            m_scr[hh] = m_next
            alphas.append(alpha)
            pvs.append(_dot(p.astype(BF16), v))
        acc_scr[...] = acc_scr[...] * jnp.where(low, alphas[0], alphas[1]) + (pvs[0] + pvs[1])

    n_full = i * (tq // tk)

    def body(j, carry):
        step(j, False)
        return carry

    lax.fori_loop(0, n_full, body, 0)
    for r in range(tq // tk):
        step(n_full + r, True)
    o_ref[0] = (acc_scr[...] / jnp.where(low, l_scr[0], l_scr[1])).astype(o_ref.dtype)


def _flash(q, k, v, sel=None):
    moba = sel is not None
    B, _, S, _ = k.shape
    tq = ATTN_TQ
    kv_spec = pl.BlockSpec((1, 2, S, LANES), lambda b, p, i: (b, p, 0, 0))
    if moba:
        q_spec = pl.BlockSpec((1, tq, LANES), lambda b, p, i: (b, i, p))
        extra = [pl.BlockSpec((1, 2, tq, sel.shape[3]), lambda b, p, i: (b, p, i, 0))]
        args = (q, k, v, sel)
    else:
        q_spec = pl.BlockSpec((1, 2, tq, LANES), lambda b, p, i: (b, p, i, 0))
        extra = []
        args = (q, k, v)
    return pl.pallas_call(
        functools.partial(_flash_kernel, moba=moba, paired_q=moba),
        grid=(B, HEAD_PAIRS, S // tq),
        in_specs=[q_spec, kv_spec, kv_spec] + extra,
        out_specs=pl.BlockSpec((1, tq, LANES), lambda b, p, i: (b, i, p)),
        out_shape=jax.ShapeDtypeStruct((B, S, HEAD_PAIRS * LANES), BF16),
        scratch_shapes=[pltpu.VMEM((2, tq, LANES), F32), pltpu.VMEM((2, tq, LANES), F32),
                        pltpu.VMEM((tq, LANES), F32)],
        compiler_params=_params("parallel", "parallel", "arbitrary"),
        name="flash_moba" if moba else "flash_mla",
    )(*args)


def _attn_out_kernel(h_ref, oa_ref, ob_ref, w_ref, o_ref):
    half = oa_ref.shape[1]
    o_ref[...] = h_ref[...] + (_dot(oa_ref[...], w_ref[0:half, :]) + _dot(ob_ref[...], w_ref[half:, :]))


def _attn_out(h2, o_mla, o_moba, w_out):
    T, D = h2.shape
    tm = 512
    half = o_mla.shape[1]
    return pl.pallas_call(
        _attn_out_kernel,
        grid=(T // tm,),
        in_specs=[pl.BlockSpec((tm, D), lambda i: (i, 0)),
                  pl.BlockSpec((tm, half), lambda i: (i, 0)),
                  pl.BlockSpec((tm, half), lambda i: (i, 0)),
                  pl.BlockSpec(w_out.shape, lambda i: (0, 0))],
        out_specs=pl.BlockSpec((tm, D), lambda i: (i, 0)),
        out_shape=jax.ShapeDtypeStruct((T, D), F32),
        compiler_params=_params("parallel"),
        name="attn_out",
    )(h2, o_mla, o_moba, w_out)


def _attn_weights(w_in, q_norm, w_uq, kv_norm, w_ukv, w_out):
    D = w_in.shape[0]
    w_cq, w_ckv = w_in[:, 0:384], w_in[:, 384:640]
    w_kr = w_in[:, 640:672]
    w_mq, w_mk, w_mv = w_in[:, 672:1184], w_in[:, 1184:1696], w_in[:, 1696:2208]
    z = lambda n: jnp.zeros((D, n), F32)
    kr1 = jnp.concatenate([z(64), w_kr, z(32)], axis=1)
    kr2 = jnp.concatenate([z(64), -w_kr[:, 16:32], w_kr[:, 0:16], z(32)], axis=1)
    w1 = jnp.concatenate([w_cq, w_ckv, kr1, kr2, w_mv], axis=1).astype(BF16)

    def rot(w):
        w4 = w.reshape(D, MOBA_HEADS, 2, MOBA_HEAD_DIM // 2)
        return jnp.stack([-w4[:, :, 1], w4[:, :, 0]], axis=2).reshape(D, MOBA_HEADS * MOBA_HEAD_DIM)

    w2 = jnp.concatenate([w_mq, rot(w_mq), w_mk, rot(w_mk)], axis=1)
    w2h = w2.astype(BF16)
    w2l = (w2 - w2h.astype(F32)).astype(BF16)

    uq = w_uq.reshape(MLA_Q_LORA, MLA_HEADS, MLA_NOPE + MLA_ROPE)
    nope, pe1, pe2 = uq[..., :64], uq[..., 64:80], uq[..., 80:96]
    zq = lambda n: jnp.zeros((MLA_Q_LORA, MLA_HEADS, n), F32)
    wq = jnp.concatenate([nope, pe1, pe2, zq(32)], axis=-1).reshape(MLA_Q_LORA, 1024).astype(BF16)
    wqr = jnp.concatenate([zq(64), -pe2, pe1, zq(32)], axis=-1).reshape(MLA_Q_LORA, 1024).astype(BF16)

    ukv = w_ukv.reshape(MLA_KV_LORA, MLA_HEADS, MLA_NOPE + MLA_V)
    k_nope, v = ukv[..., :64], ukv[..., 64:]
    zk = jnp.zeros((MLA_KV_LORA, MLA_HEADS, 64), F32)
    wk = jnp.concatenate([k_nope, zk], axis=-1).reshape(MLA_KV_LORA, 1024).astype(BF16)
    even = (jnp.arange(MLA_HEADS) % 2 == 0)[None, :, None]
    wv = jnp.concatenate([jnp.where(even, v, 0.0), jnp.where(even, 0.0, v)], axis=-1)
    wv = wv.reshape(MLA_KV_LORA, 1024).astype(BF16)
    return dict(w1=w1, w2h=w2h, w2l=w2l, gq=q_norm[None, :], gkv=kv_norm[None, :], wq=wq, wqr=wqr,
                wk=wk, wv=wv, w_out=w_out.astype(BF16))


def _attention_layer(h, g, w, tabs):
    B, S, D = h.shape
    qm, km, vm, qb, qf, kf, kb, vb = _attn_prep(h, g[None, :], w, tabs)
    o_mla = _flash(qm, km, vm)
    sel = _moba_gate(qf, kf)
    o_moba = _flash(qb, kb, vb, sel)
    out = _attn_out(h.reshape(B * S, D), o_mla.reshape(B * S, -1), o_moba.reshape(B * S, -1), w["w_out"])
    return out.reshape(B, S, D)


def _conv_in_kernel(h_ref, g_ref, wu_ref, wg_ref, bu_ref, bg_ref, o_ref):
    xn = _rms(h_ref[...], g_ref[...]).astype(BF16)
    u = _dot(xn, wu_ref[...]) + bu_ref[...]
    gate = _dot(xn, wg_ref[...]) + bg_ref[...]
    o_ref[...] = u * jax.nn.sigmoid(gate)


def _conv_in(h2, g, w_in, b_in):
    T, D = h2.shape
    tm = 512
    const = lambda a: pl.BlockSpec(a.shape, lambda i: (0, 0))
    args = [g[None, :], w_in[:, :D].astype(BF16), w_in[:, D:].astype(BF16), b_in[None, :D], b_in[None, D:]]
    return pl.pallas_call(
        _conv_in_kernel,
        grid=(T // tm,),
        in_specs=[pl.BlockSpec((tm, D), lambda i: (i, 0))] + [const(a) for a in args],
        out_specs=pl.BlockSpec((tm, D), lambda i: (i, 0)),
        out_shape=jax.ShapeDtypeStruct((T, D), F32),
        compiler_params=_params("parallel"),
        name="conv_in",
    )(h2, *args)


def _conv_main_kernel(u_ref, up_ref, h_ref, wdw_ref, bdw_ref, lg_ref, lb_ref, wo_ref, bo_ref, o_ref,
                      buf_scr, y_scr):
    i = pl.program_id(1)
    ts = u_ref.shape[1]
    buf_scr[0:CONV_HALO, :] = jnp.where(i == 0, 0.0, up_ref[0])
    buf_scr[CONV_HALO:CONV_HALO + ts, :] = u_ref[0]
    lead = CONV_HALO - (CONV_WIDTH - 1)
    for r in range(ts // CONV_ROWS):
        base = r * CONV_ROWS + lead
        acc = jnp.broadcast_to(bdw_ref[...], (CONV_ROWS, bdw_ref.shape[1]))
        for w in range(CONV_WIDTH):
            acc = acc + buf_scr[base + w:base + w + CONV_ROWS, :] * wdw_ref[w:w + 1, :]
        mu = jnp.mean(acc, axis=-1, keepdims=True)
        cen = acc - mu
        var = jnp.mean(cen * cen, axis=-1, keepdims=True)
        y = cen * lax.rsqrt(var + NORM_EPS) * lg_ref[...] + lb_ref[...]
        y_scr[r * CONV_ROWS:(r + 1) * CONV_ROWS, :] = (y * jax.nn.sigmoid(y)).astype(BF16)
    o_ref[0] = h_ref[0] + (_dot(y_scr[...], wo_ref[...]) + bo_ref[...])


def _conv_main(u, h, w_dw, b_dw, ln_g, ln_b, w_out, b_out):
    B, S, D = h.shape
    ts = CONV_TS
    per = ts // CONV_HALO
    wdw = jnp.concatenate([w_dw, jnp.zeros((1, D), F32)], axis=0)
    const = lambda a: pl.BlockSpec(a.shape, lambda b, i: (0, 0))
    args = [wdw, b_dw[None, :], ln_g[None, :], ln_b[None, :], w_out.astype(BF16), b_out[None, :]]
    tile = pl.BlockSpec((1, ts, D), lambda b, i: (b, i, 0))
    return pl.pallas_call(
        _conv_main_kernel,
        grid=(B, S // ts),
        in_specs=[tile,
                  pl.BlockSpec((1, CONV_HALO, D), lambda b, i: (b, jnp.maximum(i * per - 1, 0), 0)),
                  tile] + [const(a) for a in args],
        out_specs=tile,
        out_shape=jax.ShapeDtypeStruct((B, S, D), F32),
        scratch_shapes=[pltpu.VMEM((CONV_HALO + ts, D), F32), pltpu.VMEM((ts, D), BF16)],
        compiler_params=_params("parallel", "arbitrary"),
        name="conv_main",
    )(u, u, h, *args)


def _conv_layer(h, g, w_in, b_in, w_dw, b_dw, ln_g, ln_b, w_out, b_out):
    B, S, D = h.shape
    u = _conv_in(h.reshape(B * S, D), g, w_in, b_in).reshape(B, S, D)
    return _conv_main(u, h, w_dw, b_dw, ln_g, ln_b, w_out, b_out)


def _col_max(x):
    return jnp.max(x, axis=0, keepdims=True)


def _peer_route_kernel(h_ref, g_ref, wqh_ref, wql_ref, skh_ref, skl_ref,
                       xn_ref, beta_ref, e1_ref, s2_ref, e2_ref, q_scr, top_scr):
    xn = _rms(h_ref[...], g_ref[...])
    xh, xl = _split(xn)
    xn_ref[...] = xh
    q = _dot3(xh, xl, wqh_ref[...], wql_ref[...])
    for hc in range(2 * PEER_HEADS):
        q_scr[hc] = q[:, hc * PEER_D_HALF:(hc + 1) * PEER_D_HALF]

    def head(h, carry):
        s = []
        for c in range(2):
            qh, ql = _split(q_scr[2 * h + c])
            sc = _dot3_nt(skh_ref[2 * h + c], skl_ref[2 * h + c], qh, ql)
            s.append(sc)
            rest = sc
            for k in range(PEER_TOPK):
                top = _col_max(rest)
                top_scr[c, k:k + 1, :] = top
                if k + 1 < PEER_TOPK:
                    rest = jnp.where(rest == top, -jnp.inf, rest)
        a = top_scr[0]
        b = top_scr[1]
        cands = [a[0:1] + b, a[1:2] + b[0:8]]
        cands += [a[i:i + 1] + b[0:8] for i in range(2, 8)]
        cands += [a[8:16] + b[0:1]]
        m_top = a[0:1] + b[0:1]
        z = jnp.zeros_like(m_top)
        thr = m_top
        for k in range(PEER_TOPK):
            part = jnp.maximum(cands[0][0:8], cands[0][8:16])
            for cnd in cands[1:]:
                part = jnp.maximum(part, cnd)
            thr = _col_max(part)
            z = z + jnp.exp(thr - m_top)
            if k + 1 < PEER_TOPK:
                cands = [jnp.where(cnd == thr, -jnp.inf, cnd) for cnd in cands]
        beta = jnp.full(s[0].shape, jnp.inf, F32)
        for j in range(PEER_TOPK):
            bj = b[j:j + 1]
            beta = jnp.where(s[0] + bj >= thr, bj, beta)
        beta_ref[h] = beta
        e1_ref[h] = jnp.exp(s[0] - a[0:1]) * (1.0 / z)
        s2_ref[h] = s[1]
        e2_ref[h] = jnp.exp(s[1] - b[0:1])
        return carry

    lax.fori_loop(0, PEER_HEADS, head, 0)


def _peer_route(h2, g, wqh, wql, skh, skl):
    T, D = h2.shape
    tm = PEER_ROUTE_TM
    const = lambda a: pl.BlockSpec(a.shape, lambda i: (0,) * a.ndim)
    stat = pl.BlockSpec((PEER_HEADS, PEER_N_KEYS, tm), lambda i: (0, 0, i))
    stat_shape = jax.ShapeDtypeStruct((PEER_HEADS, PEER_N_KEYS, T), F32)
    args = [g[None, :], wqh, wql, skh, skl]
    return pl.pallas_call(
        _peer_route_kernel,
        grid=(T // tm,),
        in_specs=[pl.BlockSpec((tm, D), lambda i: (i, 0))] + [const(a) for a in args],
        out_specs=[pl.BlockSpec((tm, D), lambda i: (i, 0)), stat, stat, stat, stat],
        out_shape=[jax.ShapeDtypeStruct((T, D), BF16), stat_shape, stat_shape, stat_shape, stat_shape],
        scratch_shapes=[pltpu.VMEM((2 * PEER_HEADS, tm, PEER_D_HALF), F32),
                        pltpu.VMEM((2, PEER_TOPK, tm), F32)],
        compiler_params=_params("parallel"),
        name="peer_route",
    )(h2, *args)


def _gelu(x):
    return 0.5 * x * (1.0 + lax.erf(x * (2.0 ** -0.5)))


def _peer_main_kernel(xn_ref, u_ref, vt_ref, beta_ref, e1_ref, s2_ref, e2_ref, h_ref, o_ref,
                      acc_scr, st_scr, w_scr):
    k = pl.program_id(1)

    @pl.when(k == 0)
    def _():
        acc_scr[...] = jnp.zeros(acc_scr.shape, F32)

    st_scr[...] = _dot_nt(u_ref[...], xn_ref[...])

    def slab(a, carry):
        rows = pl.ds(pl.multiple_of(a * PEER_N_KEYS, PEER_N_KEYS), PEER_N_KEYS)
        gsum = jnp.zeros((PEER_N_KEYS, st_scr.shape[1]), F32)
        for h in range(PEER_HEADS):
            beta = beta_ref[h, pl.ds(a, 1), :]
            e1 = e1_ref[h, pl.ds(a, 1), :]
            gsum = gsum + jnp.where(s2_ref[h] >= beta, e2_ref[h] * e1, 0.0)
        w_scr[rows, :] = (gsum * _gelu(st_scr[rows, :])).astype(BF16)
        return carry

    lax.fori_loop(0, PEER_A_TILE, slab, 0)
    acc_scr[...] += _dot(vt_ref[...], w_scr[...])

    @pl.when(k == pl.num_programs(1) - 1)
    def _():
        o_ref[...] = h_ref[...] + acc_scr[...].T


def _peer_main(h2, xn, u_bf, vt_bf, beta, e1, s2, e2):
    T, D = h2.shape
    E = u_bf.shape[0]
    tm = PEER_TM
    te = PEER_A_TILE * PEER_N_KEYS
    tok = pl.BlockSpec((tm, D), lambda i, k: (i, 0))
    a_stat = pl.BlockSpec((PEER_HEADS, PEER_A_TILE, tm), lambda i, k: (0, k, i))
    b_stat = pl.BlockSpec((PEER_HEADS, PEER_N_KEYS, tm), lambda i, k: (0, 0, i))
    return pl.pallas_call(
        _peer_main_kernel,
        grid=(T // tm, E // te),
        in_specs=[tok,
                  pl.BlockSpec((te, D), lambda i, k: (k, 0)),
                  pl.BlockSpec((D, te), lambda i, k: (0, k)),
                  a_stat, a_stat, b_stat, b_stat, tok],
        out_specs=tok,
        out_shape=jax.ShapeDtypeStruct((T, D), F32),
        scratch_shapes=[pltpu.VMEM((D, tm), F32), pltpu.VMEM((te, tm), F32), pltpu.VMEM((te, tm), BF16)],
        compiler_params=_params("parallel", "arbitrary"),
        name="peer_main",
    )(xn, u_bf, vt_bf, beta, e1, s2, e2, h2)


def _peer_layer(h, g, w_q, subkeys, u_tab, v_tab):
    B, S, D = h.shape
    h2 = h.reshape(B * S, D)
    wqh = w_q.astype(BF16)
    wql = (w_q - wqh.astype(F32)).astype(BF16)
    sk = subkeys.reshape(2 * PEER_HEADS, PEER_N_KEYS, PEER_D_HALF)
    skh = sk.astype(BF16)
    skl = (sk - skh.astype(F32)).astype(BF16)
    xn, beta, e1, s2, e2 = _peer_route(h2, g, wqh, wql, skh, skl)
    out = _peer_main(h2, xn, u_tab.astype(BF16), v_tab.T.astype(BF16), beta, e1, s2, e2)
    return out.reshape(B, S, D)


def _final_norm_kernel(h_ref, g_ref, o_ref):
    o_ref[...] = _rms(h_ref[...], g_ref[...])


def _final_norm(h2, g):
    T, D = h2.shape
    tm = 512
    return pl.pallas_call(
        _final_norm_kernel,
        grid=(T // tm,),
        in_specs=[pl.BlockSpec((tm, D), lambda i: (i, 0)), pl.BlockSpec((1, D), lambda i: (0, 0))],
        out_specs=pl.BlockSpec((tm, D), lambda i: (i, 0)),
        out_shape=jax.ShapeDtypeStruct((T, D), F32),
        compiler_params=_params("parallel"),
        name="final_norm",
    )(h2, g[None, :])


def kernel(x, positions, norm_mix, norm_ffn, norm_final, attn_w_in, mla_q_norm, mla_w_uq, mla_kv_norm,
           mla_w_ukv, attn_w_out, conv_w_in, conv_b_in, conv_w_dw, conv_b_dw, conv_ln_g, conv_ln_b,
           conv_w_out, conv_b_out, peer_w_q, peer_subkeys, peer_u, peer_v):
    B, S, D = x.shape
    assert D == D_MODEL and S % ATTN_TQ == 0 and S % MOBA_BLOCK == 0 and (B * S) % PEER_TM == 0
    depth = norm_mix.shape[0]
    tabs = _rope_tables(positions)
    h = x
    for layer in range(depth):
        i = layer // 2
        if layer % 2 == 0:
            w = _attn_weights(attn_w_in[i], mla_q_norm[i], mla_w_uq[i], mla_kv_norm[i], mla_w_ukv[i],
                              attn_w_out[i])
            h = _attention_layer(h, norm_mix[layer], w, tabs)
        else:
            h = _conv_layer(h, norm_mix[layer], conv_w_in[i], conv_b_in[i], conv_w_dw[i], conv_b_dw[i],
                            conv_ln_g[i], conv_ln_b[i], conv_w_out[i], conv_b_out[i])
        h = _peer_layer(h, norm_ffn[layer], peer_w_q[layer], peer_subkeys[layer], peer_u[layer],
                        peer_v[layer])
    return _final_norm(h.reshape(B * S, D), norm_final).reshape(B, S, D)
```

```python
import functools

import jax
import jax.numpy as jnp
import numpy as np
from jax import lax
from jax.experimental import pallas as pl
from jax.experimental.pallas import tpu as pltpu

F32 = jnp.float32
BF16 = jnp.bfloat16

D_MODEL = 1024
ROPE_THETA = 10000.0
NORM_EPS = 1e-6

MLA_HEADS = 8
MLA_Q_LORA = 384
MLA_KV_LORA = 256
MLA_NOPE = 64
MLA_ROPE = 32
MLA_V = 64

MOBA_HEADS = 8
MOBA_HEAD_DIM = 64
MOBA_BLOCK = 256
MOBA_TOPK = 3

CONV_WIDTH = 31

PEER_HEADS = 8
PEER_N_KEYS = 128
PEER_D_HALF = 128
PEER_TOPK = 16

LANES = 128
HEAD_PAIRS = 4
VMEM_LIMIT = 56 * 1024 * 1024
NEG = -1e30

ATTN_TQ = 512
ATTN_TK = 512
PROJ_TM = 256
PEER_ROUTE_TM = 256
PEER_TM = 512
PEER_A_TILE = 8
NOT_RANKED = 255.0
CONV_TS = 512
CONV_HALO = 32
CONV_ROWS = 32


def _params(*sem):
    return pltpu.CompilerParams(dimension_semantics=sem, vmem_limit_bytes=VMEM_LIMIT)


def _dot(a, b):
    return jnp.dot(a, b, preferred_element_type=F32)


def _dot_nt(a, b):
    return lax.dot_general(a, b, (((1,), (1,)), ((), ())), preferred_element_type=F32)


def _split(a):
    hi = a.astype(BF16)
    lo = (a - hi.astype(F32)).astype(BF16)
    return hi, lo


def _dot3(ah, al, bh, bl):
    return _dot(ah, bh) + (_dot(ah, bl) + _dot(al, bh))


def _dot3_nt(ah, al, bh, bl):
    return _dot_nt(ah, bh) + (_dot_nt(ah, bl) + _dot_nt(al, bh))


def _rms(x, g):
    return x * lax.rsqrt(jnp.mean(x * x, axis=-1, keepdims=True) + NORM_EPS) * g


def _rope_kernel(pos_ref, fr_ref, fm_ref, cr_ref, sr_ref, cm_ref, sm_ref):
    pos = pos_ref[0].astype(F32)
    ang_r = pos * fr_ref[0:1, :]
    ang_m = pos * fm_ref[0:1, :]
    cr_ref[0] = jnp.cos(ang_r) * fr_ref[1:2, :] + fr_ref[2:3, :]
    sr_ref[0] = jnp.sin(ang_r) * fr_ref[1:2, :]
    cm_ref[0] = jnp.cos(ang_m)
    sm_ref[0] = jnp.sin(ang_m)


def _rope_tables(positions):
    B, S = positions.shape
    ts = 512
    inv_r = 1.0 / (ROPE_THETA ** (jnp.arange(0, MLA_ROPE, 2, dtype=F32) / MLA_ROPE))
    inv_m = 1.0 / (ROPE_THETA ** (jnp.arange(0, MOBA_HEAD_DIM, 2, dtype=F32) / MOBA_HEAD_DIM))
    z32 = jnp.zeros((32,), F32)
    z64 = jnp.zeros((64,), F32)
    fr = jnp.stack([jnp.concatenate([z64, inv_r, inv_r, z32]),
                    jnp.concatenate([z64, jnp.ones((32,), F32), z32]),
                    jnp.concatenate([jnp.ones((64,), F32), z32, z32])])
    fr = jnp.concatenate([fr, jnp.zeros((5, LANES), F32)])
    fm = jnp.concatenate([jnp.tile(inv_m, 4)[None, :], jnp.zeros((7, LANES), F32)])
    tab = jax.ShapeDtypeStruct((B, S, LANES), F32)
    row = pl.BlockSpec((1, ts, LANES), lambda b, i: (b, i, 0))
    return pl.pallas_call(
        _rope_kernel,
        grid=(B, S // ts),
        in_specs=[pl.BlockSpec((1, ts, 1), lambda b, i: (b, i, 0)),
                  pl.BlockSpec((8, LANES), lambda b, i: (0, 0)),
                  pl.BlockSpec((8, LANES), lambda b, i: (0, 0))],
        out_specs=[row, row, row, row],
        out_shape=[tab, tab, tab, tab],
        compiler_params=_params("parallel", "parallel"),
        name="rope_tables",
    )(positions.reshape(B, S, 1), fr, fm)


def _attn_prep_kernel(h_ref, g_ref, w1_ref, w2h_ref, w2l_ref, gq_ref, gkv_ref, wq_ref, wqr_ref,
                      wk_ref, wv_ref, cr_ref, sr_ref, cm_ref, sm_ref,
                      qm_ref, km_ref, vm_ref, qb_ref, qf_ref, kf_ref, kb_ref, vb_ref):
    xn = _rms(h_ref[0], g_ref[...])
    xh, xl = _split(xn)
    p1 = _dot(xh, w1_ref[...])
    p2 = _dot3(xh, xl, w2h_ref[...], w2l_ref[...])
    cq, ckv = p1[:, 0:384], p1[:, 384:640]
    kr1, kr2 = p1[:, 640:768], p1[:, 768:896]
    mv = p1[:, 896:1408]
    cr, sr, cm, sm = cr_ref[0], sr_ref[0], cm_ref[0], sm_ref[0]
    nq = _rms(cq, gq_ref[...]).astype(BF16)
    nkv = _rms(ckv, gkv_ref[...]).astype(BF16)
    qa = _dot(nq, wq_ref[...])
    qr = _dot(nq, wqr_ref[...])
    ka = _dot(nkv, wk_ref[...])
    va = _dot(nkv, wv_ref[...])
    kpe = kr1 * cr + kr2 * sr
    mla_scale = (MLA_NOPE + MLA_ROPE) ** -0.5
    for h in range(MLA_HEADS):
        sl = slice(h * LANES, (h + 1) * LANES)
        qm_ref[0, h] = ((qa[:, sl] * cr + qr[:, sl] * sr) * mla_scale).astype(BF16)
        km_ref[0, h] = (ka[:, sl] + kpe).astype(BF16)
        vm_ref[0, h] = va[:, sl].astype(BF16)
    low = lax.broadcasted_iota(jnp.int32, cm.shape, 1) < MOBA_HEAD_DIM
    moba_scale = MOBA_HEAD_DIM ** -0.5
    for p in range(HEAD_PAIRS):
        sl = slice(p * LANES, (p + 1) * LANES)
        q = p2[:, sl] * cm + p2[:, 512 + p * LANES:512 + (p + 1) * LANES] * sm
        k = p2[:, 1024 + p * LANES:1024 + (p + 1) * LANES] * cm + p2[:, 1536 + p * LANES:1536 + (p + 1) * LANES] * sm
        v = mv[:, sl]
        qf_ref[0, :, sl] = q
        kf_ref[0, :, sl] = k
        qb_ref[0, :, sl] = (q * moba_scale).astype(BF16)
        kb_ref[0, 2 * p] = jnp.where(low, k, 0.0).astype(BF16)
        kb_ref[0, 2 * p + 1] = jnp.where(low, 0.0, k).astype(BF16)
        vb_ref[0, 2 * p] = jnp.where(low, v, 0.0).astype(BF16)
        vb_ref[0, 2 * p + 1] = jnp.where(low, 0.0, v).astype(BF16)


def _attn_prep(h, g, w, tabs):
    B, S, D = h.shape
    tm = PROJ_TM
    const = lambda a: pl.BlockSpec(a.shape, lambda b, i: (0,) * a.ndim)
    row = lambda n: pl.BlockSpec((1, tm, n), lambda b, i: (b, i, 0))
    head = pl.BlockSpec((1, 8, tm, LANES), lambda b, i: (b, 0, i, 0))
    head_shape = jax.ShapeDtypeStruct((B, 8, S, LANES), BF16)
    weights = [g, w["w1"], w["w2h"], w["w2l"], w["gq"], w["gkv"], w["wq"], w["wqr"], w["wk"], w["wv"]]
    return pl.pallas_call(
        _attn_prep_kernel,
        grid=(B, S // tm),
        in_specs=[row(D)] + [const(a) for a in weights] + [row(LANES)] * 4,
        out_specs=[head, head, head, row(512), row(512), row(512), head, head],
        out_shape=[head_shape, head_shape, head_shape,
                   jax.ShapeDtypeStruct((B, S, 512), BF16),
                   jax.ShapeDtypeStruct((B, S, 512), F32),
                   jax.ShapeDtypeStruct((B, S, 512), F32),
                   head_shape, head_shape],
        compiler_params=_params("parallel", "parallel"),
        name="attn_prep",
    )(h, *weights, *tabs)


def _moba_gate_kernel(q_ref, k_ref, sel_ref, kmh_ref, kml_ref):
    i = pl.program_id(2)
    tq = q_ref.shape[1]
    S = k_ref.shape[1]
    nb = S // MOBA_BLOCK

    @pl.when(i == 0)
    def _():
        km = jnp.sum(k_ref[0].reshape(nb, MOBA_BLOCK, LANES), axis=1) / float(MOBA_BLOCK)
        low = lax.broadcasted_iota(jnp.int32, km.shape, 1) < MOBA_HEAD_DIM
        for hh, m in enumerate((jnp.where(low, km, 0.0), jnp.where(low, 0.0, km))):
            hi, lo = _split(m)
            kmh_ref[hh] = hi
            kml_ref[hh] = lo

    qh, ql = _split(q_ref[0])
    row = i * tq + lax.broadcasted_iota(jnp.int32, (tq, nb), 0)
    blk = lax.broadcasted_iota(jnp.int32, (tq, nb), 1)
    valid = blk * MOBA_BLOCK < row - (row & (MOBA_BLOCK - 1))
    for hh in range(2):
        gate = _dot3_nt(qh, ql, kmh_ref[hh], kml_ref[hh])
        g = jnp.where(valid, gate, -jnp.inf)
        rest = g
        for _ in range(MOBA_TOPK - 1):
            top = jnp.max(rest, axis=1, keepdims=True)
            rest = jnp.where(rest == top, -jnp.inf, rest)
        thr = jnp.max(rest, axis=1, keepdims=True)
        sel_ref[0, hh] = jnp.where(valid & (g >= thr), 1.0, 0.0)


def _moba_gate(qf, kf):
    B, S, _ = qf.shape
    tq = ATTN_TQ
    nb = S // MOBA_BLOCK
    return pl.pallas_call(
        _moba_gate_kernel,
        grid=(B, HEAD_PAIRS, S // tq),
        in_specs=[pl.BlockSpec((1, tq, LANES), lambda b, p, i: (b, i, p)),
                  pl.BlockSpec((1, S, LANES), lambda b, p, i: (b, 0, p))],
        out_specs=pl.BlockSpec((1, 2, tq, nb), lambda b, p, i: (b, p, i, 0)),
        out_shape=jax.ShapeDtypeStruct((B, 8, S, nb), F32),
        scratch_shapes=[pltpu.VMEM((2, nb, LANES), BF16), pltpu.VMEM((2, nb, LANES), BF16)],
        compiler_params=_params("parallel", "parallel", "arbitrary"),
        name="moba_gate",
    )(qf, kf)


def _flash_kernel(*refs, moba, paired_q):
    if moba:
        q_ref, k_ref, v_ref, sel_ref, o_ref, m_scr, l_scr, acc_scr = refs
    else:
        q_ref, k_ref, v_ref, o_ref, m_scr, l_scr, acc_scr = refs
    i = pl.program_id(2)
    tq, tk = ATTN_TQ, ATTN_TK
    if paired_q:
        qs = (q_ref[0], q_ref[0])
    else:
        qs = (q_ref[0, 0], q_ref[0, 1])
    m_scr[...] = jnp.full(m_scr.shape, NEG, F32)
    l_scr[...] = jnp.zeros(l_scr.shape, F32)
    acc_scr[...] = jnp.zeros(acc_scr.shape, F32)
    low = lax.broadcasted_iota(jnp.int32, (tq, LANES), 1) < 64
    nsel = sel_ref.shape[3] if moba else 0

    def step(j, diag):
        start = pl.multiple_of(j * tk, tk)
        alphas, pvs = [], []
        for hh in range(2):
            k = k_ref[0, hh, pl.ds(start, tk), :]
            v = v_ref[0, hh, pl.ds(start, tk), :]
            s = _dot_nt(qs[hh], k)
            if moba:
                sel = sel_ref[0, hh]
                col = lax.broadcasted_iota(jnp.int32, sel.shape, 1)
                sub = tk // MOBA_BLOCK
                picked = jnp.concatenate(
                    [jnp.broadcast_to(
                        jnp.sum(jnp.where(col == j * sub + r, sel, 0.0), axis=1, keepdims=True),
                        (tq, MOBA_BLOCK)) for r in range(sub)], axis=1) > 0.5
            if diag:
                qpos = i * tq + lax.broadcasted_iota(jnp.int32, (tq, tk), 0)
                kpos = j * tk + lax.broadcasted_iota(jnp.int32, (tq, tk), 1)
                ok = kpos <= qpos
                if moba:
                    ok = (ok & (kpos >= qpos - (qpos & (MOBA_BLOCK - 1)))) | picked
                s = jnp.where(ok, s, NEG)
            elif moba:
                s = jnp.where(picked, s, NEG)
            m_prev = m_scr[hh]
            m_next = jnp.maximum(m_prev, jnp.max(s, axis=1, keepdims=True))
            alpha = jnp.exp(m_prev - m_next)
            p = jnp.exp(s - jnp.tile(m_next, (1, tk // LANES)))
            l_scr[hh] = alpha * l_scr[hh] + jnp.sum(p, axis=1, keepdims=True)
            m_scr[hh] = m_next
            alphas.append(alpha)
            pvs.append(_dot(p.astype(BF16), v))
        acc_scr[...] = acc_scr[...] * jnp.where(low, alphas[0], alphas[1]) + (pvs[0] + pvs[1])

    n_full = i * (tq // tk)

    def body(j, carry):
        step(j, False)
        return carry

    lax.fori_loop(0, n_full, body, 0)
    for r in range(tq // tk):
        step(n_full + r, True)
    o_ref[0] = (acc_scr[...] / jnp.where(low, l_scr[0], l_scr[1])).astype(o_ref.dtype)


def _flash(q, k, v, sel=None):
    moba = sel is not None
    B, _, S, _ = k.shape
    tq = ATTN_TQ
    kv_spec = pl.BlockSpec((1, 2, S, LANES), lambda b, p, i: (b, p, 0, 0))
    if moba:
        q_spec = pl.BlockSpec((1, tq, LANES), lambda b, p, i: (b, i, p))
        extra = [pl.BlockSpec((1, 2, tq, sel.shape[3]), lambda b, p, i: (b, p, i, 0))]
        args = (q, k, v, sel)
    else:
        q_spec = pl.BlockSpec((1, 2, tq, LANES), lambda b, p, i: (b, p, i, 0))
        extra = []
        args = (q, k, v)
    return pl.pallas_call(
        functools.partial(_flash_kernel, moba=moba, paired_q=moba),
        grid=(B, HEAD_PAIRS, S // tq),
        in_specs=[q_spec, kv_spec, kv_spec] + extra,
        out_specs=pl.BlockSpec((1, tq, LANES), lambda b, p, i: (b, i, p)),
        out_shape=jax.ShapeDtypeStruct((B, S, HEAD_PAIRS * LANES), BF16),
        scratch_shapes=[pltpu.VMEM((2, tq, LANES), F32), pltpu.VMEM((2, tq, LANES), F32),
                        pltpu.VMEM((tq, LANES), F32)],
        compiler_params=_params("parallel", "parallel", "arbitrary"),
        name="flash_moba" if moba else "flash_mla",
    )(*args)


def _attn_out_kernel(h_ref, oa_ref, ob_ref, w_ref, o_ref):
    half = oa_ref.shape[1]
    o_ref[...] = h_ref[...] + (_dot(oa_ref[...], w_ref[0:half, :]) + _dot(ob_ref[...], w_ref[half:, :]))


def _attn_out(h2, o_mla, o_moba, w_out):
    T, D = h2.shape
    tm = 512
    half = o_mla.shape[1]
    return pl.pallas_call(
        _attn_out_kernel,
        grid=(T // tm,),
        in_specs=[pl.BlockSpec((tm, D), lambda i: (i, 0)),
                  pl.BlockSpec((tm, half), lambda i: (i, 0)),
                  pl.BlockSpec((tm, half), lambda i: (i, 0)),
                  pl.BlockSpec(w_out.shape, lambda i: (0, 0))],
        out_specs=pl.BlockSpec((tm, D), lambda i: (i, 0)),
        out_shape=jax.ShapeDtypeStruct((T, D), F32),
        compiler_params=_params("parallel"),
        name="attn_out",
    )(h2, o_mla, o_moba, w_out)


def _attn_weights(w_in, q_norm, w_uq, kv_norm, w_ukv, w_out):
    D = w_in.shape[0]
    w_cq, w_ckv = w_in[:, 0:384], w_in[:, 384:640]
    w_kr = w_in[:, 640:672]
    w_mq, w_mk, w_mv = w_in[:, 672:1184], w_in[:, 1184:1696], w_in[:, 1696:2208]
    z = lambda n: jnp.zeros((D, n), F32)
    kr1 = jnp.concatenate([z(64), w_kr, z(32)], axis=1)
    kr2 = jnp.concatenate([z(64), -w_kr[:, 16:32], w_kr[:, 0:16], z(32)], axis=1)
    w1 = jnp.concatenate([w_cq, w_ckv, kr1, kr2, w_mv], axis=1).astype(BF16)

    def rot(w):
        w4 = w.reshape(D, MOBA_HEADS, 2, MOBA_HEAD_DIM // 2)
        return jnp.stack([-w4[:, :, 1], w4[:, :, 0]], axis=2).reshape(D, MOBA_HEADS * MOBA_HEAD_DIM)

    w2 = jnp.concatenate([w_mq, rot(w_mq), w_mk, rot(w_mk)], axis=1)
    w2h = w2.astype(BF16)
    w2l = (w2 - w2h.astype(F32)).astype(BF16)

    uq = w_uq.reshape(MLA_Q_LORA, MLA_HEADS, MLA_NOPE + MLA_ROPE)
    nope, pe1, pe2 = uq[..., :64], uq[..., 64:80], uq[..., 80:96]
    zq = lambda n: jnp.zeros((MLA_Q_LORA, MLA_HEADS, n), F32)
    wq = jnp.concatenate([nope, pe1, pe2, zq(32)], axis=-1).reshape(MLA_Q_LORA, 1024).astype(BF16)
    wqr = jnp.concatenate([zq(64), -pe2, pe1, zq(32)], axis=-1).reshape(MLA_Q_LORA, 1024).astype(BF16)

    ukv = w_ukv.reshape(MLA_KV_LORA, MLA_HEADS, MLA_NOPE + MLA_V)
    k_nope, v = ukv[..., :64], ukv[..., 64:]
    zk = jnp.zeros((MLA_KV_LORA, MLA_HEADS, 64), F32)
    wk = jnp.concatenate([k_nope, zk], axis=-1).reshape(MLA_KV_LORA, 1024).astype(BF16)
    even = (jnp.arange(MLA_HEADS) % 2 == 0)[None, :, None]
    wv = jnp.concatenate([jnp.where(even, v, 0.0), jnp.where(even, 0.0, v)], axis=-1)
    wv = wv.reshape(MLA_KV_LORA, 1024).astype(BF16)
    return dict(w1=w1, w2h=w2h, w2l=w2l, gq=q_norm[None, :], gkv=kv_norm[None, :], wq=wq, wqr=wqr,
                wk=wk, wv=wv, w_out=w_out.astype(BF16))


def _attention_layer(h, g, w, tabs):
    B, S, D = h.shape
    qm, km, vm, qb, qf, kf, kb, vb = _attn_prep(h, g[None, :], w, tabs)
    o_mla = _flash(qm, km, vm)
    sel = _moba_gate(qf, kf)
    o_moba = _flash(qb, kb, vb, sel)
    out = _attn_out(h.reshape(B * S, D), o_mla.reshape(B * S, -1), o_moba.reshape(B * S, -1), w["w_out"])
    return out.reshape(B, S, D)


def _conv_in_kernel(h_ref, g_ref, wu_ref, wg_ref, bu_ref, bg_ref, o_ref):
    xn = _rms(h_ref[...], g_ref[...]).astype(BF16)
    u = _dot(xn, wu_ref[...]) + bu_ref[...]
    gate = _dot(xn, wg_ref[...]) + bg_ref[...]
    o_ref[...] = u * jax.nn.sigmoid(gate)


def _conv_in(h2, g, w_in, b_in):
    T, D = h2.shape
    tm = 512
    const = lambda a: pl.BlockSpec(a.shape, lambda i: (0, 0))
    args = [g[None, :], w_in[:, :D].astype(BF16), w_in[:, D:].astype(BF16), b_in[None, :D], b_in[None, D:]]
    return pl.pallas_call(
        _conv_in_kernel,
        grid=(T // tm,),
        in_specs=[pl.BlockSpec((tm, D), lambda i: (i, 0))] + [const(a) for a in args],
        out_specs=pl.BlockSpec((tm, D), lambda i: (i, 0)),
        out_shape=jax.ShapeDtypeStruct((T, D), F32),
        compiler_params=_params("parallel"),
        name="conv_in",
    )(h2, *args)


def _conv_main_kernel(u_ref, up_ref, h_ref, wdw_ref, bdw_ref, lg_ref, lb_ref, wo_ref, bo_ref, o_ref,
                      buf_scr, y_scr):
    i = pl.program_id(1)
    ts = u_ref.shape[1]
    buf_scr[0:CONV_HALO, :] = jnp.where(i == 0, 0.0, up_ref[0])
    buf_scr[CONV_HALO:CONV_HALO + ts, :] = u_ref[0]
    lead = CONV_HALO - (CONV_WIDTH - 1)
    for r in range(ts // CONV_ROWS):
        base = r * CONV_ROWS + lead
        acc = jnp.broadcast_to(bdw_ref[...], (CONV_ROWS, bdw_ref.shape[1]))
        for w in range(CONV_WIDTH):
            acc = acc + buf_scr[base + w:base + w + CONV_ROWS, :] * wdw_ref[w:w + 1, :]
        mu = jnp.mean(acc, axis=-1, keepdims=True)
        cen = acc - mu
        var = jnp.mean(cen * cen, axis=-1, keepdims=True)
        y = cen * lax.rsqrt(var + NORM_EPS) * lg_ref[...] + lb_ref[...]
        y_scr[r * CONV_ROWS:(r + 1) * CONV_ROWS, :] = (y * jax.nn.sigmoid(y)).astype(BF16)
    o_ref[0] = h_ref[0] + (_dot(y_scr[...], wo_ref[...]) + bo_ref[...])


def _conv_main(u, h, w_dw, b_dw, ln_g, ln_b, w_out, b_out):
    B, S, D = h.shape
    ts = CONV_TS
    per = ts // CONV_HALO
    wdw = jnp.concatenate([w_dw, jnp.zeros((1, D), F32)], axis=0)
    const = lambda a: pl.BlockSpec(a.shape, lambda b, i: (0, 0))
    args = [wdw, b_dw[None, :], ln_g[None, :], ln_b[None, :], w_out.astype(BF16), b_out[None, :]]
    tile = pl.BlockSpec((1, ts, D), lambda b, i: (b, i, 0))
    return pl.pallas_call(
        _conv_main_kernel,
        grid=(B, S // ts),
        in_specs=[tile,
                  pl.BlockSpec((1, CONV_HALO, D), lambda b, i: (b, jnp.maximum(i * per - 1, 0), 0)),
                  tile] + [const(a) for a in args],
        out_specs=tile,
        out_shape=jax.ShapeDtypeStruct((B, S, D), F32),
        scratch_shapes=[pltpu.VMEM((CONV_HALO + ts, D), F32), pltpu.VMEM((ts, D), BF16)],
        compiler_params=_params("parallel", "arbitrary"),
        name="conv_main",
    )(u, u, h, *args)


def _conv_layer(h, g, w_in, b_in, w_dw, b_dw, ln_g, ln_b, w_out, b_out):
    B, S, D = h.shape
    u = _conv_in(h.reshape(B * S, D), g, w_in, b_in).reshape(B, S, D)
    return _conv_main(u, h, w_dw, b_dw, ln_g, ln_b, w_out, b_out)


def _col_max(x):
    return jnp.max(x, axis=0, keepdims=True)


def _dup_bf16(x):
    bits = pltpu.bitcast(x.astype(BF16).astype(F32), jnp.uint32)
    return (bits & jnp.uint32(0xFFFF0000)) | (bits >> 16)


def _peer_route_kernel(h_ref, g_ref, wqh_ref, wql_ref, skh_ref, skl_ref,
                       xn_ref, cnt_ref, e1_ref, rank_ref, e2_ref, q_scr, top_scr):
    xn = _rms(h_ref[...], g_ref[...])
    xh, xl = _split(xn)
    xn_ref[...] = xh
    q = _dot3(xh, xl, wqh_ref[...], wql_ref[...])
    for hc in range(2 * PEER_HEADS):
        q_scr[hc] = q[:, hc * PEER_D_HALF:(hc + 1) * PEER_D_HALF]

    def head(h, carry):
        s = []
        rank = None
        for c in range(2):
            qh, ql = _split(q_scr[2 * h + c])
            sc = _dot3_nt(skh_ref[2 * h + c], skl_ref[2 * h + c], qh, ql)
            s.append(sc)
            rest = sc
            rank = jnp.full(sc.shape, NOT_RANKED, F32)
            for k in range(PEER_TOPK):
                top = _col_max(rest)
                top_scr[c, k:k + 1, :] = top
                hit = rest == top
                if c == 1:
                    rank = jnp.where(hit, float(k), rank)
                if k + 1 < PEER_TOPK:
                    rest = jnp.where(hit, -jnp.inf, rest)
        a = top_scr[0]
        b = top_scr[1]
        cands = [a[0:1] + b, a[1:2] + b[0:8]]
        cands += [a[i:i + 1] + b[0:8] for i in range(2, 8)]
        cands += [a[8:16] + b[0:1]]
        m_top = a[0:1] + b[0:1]
        z = jnp.zeros_like(m_top)
        thr = m_top
        for k in range(PEER_TOPK):
            part = jnp.maximum(cands[0][0:8], cands[0][8:16])
            for cnd in cands[1:]:
                part = jnp.maximum(part, cnd)
            thr = _col_max(part)
            z = z + jnp.exp(thr - m_top)
            if k + 1 < PEER_TOPK:
                cands = [jnp.where(cnd == thr, -jnp.inf, cnd) for cnd in cands]
        cnt = jnp.zeros(s[0].shape, F32)
        for j in range(PEER_TOPK):
            cnt = cnt + jnp.where(s[0] + b[j:j + 1] >= thr, 1.0, 0.0)
        cnt_ref[h] = _dup_bf16(cnt)
        e1_ref[h] = _dup_bf16(jnp.exp(s[0] - a[0:1]) * (1.0 / z))
        rank_ref[h] = rank.astype(BF16)
        e2_ref[h] = jnp.exp(s[1] - b[0:1]).astype(BF16)
        return carry

    lax.fori_loop(0, PEER_HEADS, head, 0)


def _peer_route(h2, g, wqh, wql, skh, skl):
    T, D = h2.shape
    tm = PEER_ROUTE_TM
    const = lambda a: pl.BlockSpec(a.shape, lambda i: (0,) * a.ndim)
    stat = pl.BlockSpec((PEER_HEADS, PEER_N_KEYS, tm), lambda i: (0, 0, i))
    a_shape = jax.ShapeDtypeStruct((PEER_HEADS, PEER_N_KEYS, T), jnp.uint32)
    b_shape = jax.ShapeDtypeStruct((PEER_HEADS, PEER_N_KEYS, T), BF16)
    args = [g[None, :], wqh, wql, skh, skl]
    return pl.pallas_call(
        _peer_route_kernel,
        grid=(T // tm,),
        in_specs=[pl.BlockSpec((tm, D), lambda i: (i, 0))] + [const(a) for a in args],
        out_specs=[pl.BlockSpec((tm, D), lambda i: (i, 0)), stat, stat, stat, stat],
        out_shape=[jax.ShapeDtypeStruct((T, D), BF16), a_shape, a_shape, b_shape, b_shape],
        scratch_shapes=[pltpu.VMEM((2 * PEER_HEADS, tm, PEER_D_HALF), F32),
                        pltpu.VMEM((2, PEER_TOPK, tm), F32)],
        compiler_params=_params("parallel"),
        name="peer_route",
    )(h2, *args)


def _gelu(x):
    return 0.5 * x * (1.0 + lax.erf(x * (2.0 ** -0.5)))


def _bcast_row(ref, h, a):
    row = ref[h, a:a + 1, :]
    return pltpu.bitcast(jnp.broadcast_to(row, (8, row.shape[1])), BF16)


def _peer_main_kernel(xn_ref, u_ref, vt_ref, cnt_ref, e1_ref, rank_ref, e2_ref, h_ref, o_ref,
                      acc_scr, st0, st1, w0, w1, *, nk, n_tiles):
    g = pl.program_id(0)
    kc = jnp.clip(g - 2, 0, n_tiles - 1) % nk
    tm = st0.shape[1]

    @pl.when(g == 0)
    def _():
        for ref in (st0, st1, w0, w1):
            ref[...] = jnp.zeros(ref.shape, ref.dtype)

    @pl.when(kc == 0)
    def _():
        acc_scr[...] = jnp.zeros(acc_scr.shape, F32)

    def stages(st_a, st_b, w_b, w_c):
        acc_scr[...] += _dot(vt_ref[...], w_c[...])
        st_a[...] = _dot_nt(u_ref[...], xn_ref[...])
        for a in range(PEER_A_TILE):
            rows = slice(a * PEER_N_KEYS, (a + 1) * PEER_N_KEYS)
            gsum = jnp.zeros((PEER_N_KEYS, tm), BF16)
            for h in range(PEER_HEADS):
                cnt = _bcast_row(cnt_ref, h, a)
                e1 = _bcast_row(e1_ref, h, a)
                reps = (PEER_N_KEYS // cnt.shape[0], 1)
                picked = rank_ref[h] < jnp.tile(cnt, reps)
                gsum = gsum + jnp.where(picked, e2_ref[h] * jnp.tile(e1, reps), jnp.zeros((), BF16))
            w_b[rows, :] = gsum * _gelu(st_b[rows, :]).astype(BF16)

    @pl.when(g % 2 == 0)
    def _():
        stages(st0, st1, w1, w0)

    @pl.when(g % 2 == 1)
    def _():
        stages(st1, st0, w0, w1)

    @pl.when((g >= 2) & (kc == nk - 1))
    def _():
        o_ref[...] = h_ref[...] + acc_scr[...].T


def _peer_main(h2, xn, u_bf, vt_bf, cnt, e1, rank, e2):
    T, D = h2.shape
    E = u_bf.shape[0]
    tm = PEER_TM
    te = PEER_A_TILE * PEER_N_KEYS
    nk = E // te
    n_tiles = (T // tm) * nk
    tile_a = lambda g: jnp.minimum(g, n_tiles - 1)
    tile_b = lambda g: jnp.clip(g - 1, 0, n_tiles - 1)
    tile_c = lambda g: jnp.clip(g - 2, 0, n_tiles - 1)
    a_stat = pl.BlockSpec((PEER_HEADS, PEER_A_TILE, tm), lambda g: (0, tile_b(g) % nk, tile_b(g) // nk))
    b_stat = pl.BlockSpec((PEER_HEADS, PEER_N_KEYS, tm), lambda g: (0, 0, tile_b(g) // nk))
    tok_c = pl.BlockSpec((tm, D), lambda g: (tile_c(g) // nk, 0))
    return pl.pallas_call(
        functools.partial(_peer_main_kernel, nk=nk, n_tiles=n_tiles),
        grid=(n_tiles + 2,),
        in_specs=[pl.BlockSpec((tm, D), lambda g: (tile_a(g) // nk, 0)),
                  pl.BlockSpec((te, D), lambda g: (tile_a(g) % nk, 0)),
                  pl.BlockSpec((D, te), lambda g: (0, tile_c(g) % nk)),
                  a_stat, a_stat, b_stat, b_stat, tok_c],
        out_specs=tok_c,
        out_shape=jax.ShapeDtypeStruct((T, D), F32),
        scratch_shapes=[pltpu.VMEM((D, tm), F32),
                        pltpu.VMEM((te, tm), F32), pltpu.VMEM((te, tm), F32),
                        pltpu.VMEM((te, tm), BF16), pltpu.VMEM((te, tm), BF16)],
        compiler_params=_params("arbitrary"),
        name="peer_main",
    )(xn, u_bf, vt_bf, cnt, e1, rank, e2, h2)


def _peer_layer(h, g, w_q, subkeys, u_tab, v_tab):
    B, S, D = h.shape
    h2 = h.reshape(B * S, D)
    wqh = w_q.astype(BF16)
    wql = (w_q - wqh.astype(F32)).astype(BF16)
    sk = subkeys.reshape(2 * PEER_HEADS, PEER_N_KEYS, PEER_D_HALF)
    skh = sk.astype(BF16)
    skl = (sk - skh.astype(F32)).astype(BF16)
    xn, cnt, e1, rank, e2 = _peer_route(h2, g, wqh, wql, skh, skl)
    out = _peer_main(h2, xn, u_tab.astype(BF16), v_tab.T.astype(BF16), cnt, e1, rank, e2)
    return out.reshape(B, S, D)


def _final_norm_kernel(h_ref, g_ref, o_ref):
    o_ref[...] = _rms(h_ref[...], g_ref[...])


def _final_norm(h2, g):
    T, D = h2.shape
    tm = 512
    return pl.pallas_call(
        _final_norm_kernel,
        grid=(T // tm,),
        in_specs=[pl.BlockSpec((tm, D), lambda i: (i, 0)), pl.BlockSpec((1, D), lambda i: (0, 0))],
        out_specs=pl.BlockSpec((tm, D), lambda i: (i, 0)),
        out_shape=jax.ShapeDtypeStruct((T, D), F32),
        compiler_params=_params("parallel"),
        name="final_norm",
    )(h2, g[None, :])


def kernel(x, positions, norm_mix, norm_ffn, norm_final, attn_w_in, mla_q_norm, mla_w_uq, mla_kv_norm,
           mla_w_ukv, attn_w_out, conv_w_in, conv_b_in, conv_w_dw, conv_b_dw, conv_ln_g, conv_ln_b,
           conv_w_out, conv_b_out, peer_w_q, peer_subkeys, peer_u, peer_v):
    B, S, D = x.shape
    assert D == D_MODEL and S % ATTN_TQ == 0 and S % MOBA_BLOCK == 0 and (B * S) % PEER_TM == 0
    depth = norm_mix.shape[0]
    tabs = _rope_tables(positions)
    h = x
    for layer in range(depth):
        i = layer // 2
        if layer % 2 == 0:
            w = _attn_weights(attn_w_in[i], mla_q_norm[i], mla_w_uq[i], mla_kv_norm[i], mla_w_ukv[i],
                              attn_w_out[i])
            h = _attention_layer(h, norm_mix[layer], w, tabs)
        else:
            h = _conv_layer(h, norm_mix[layer], conv_w_in[i], conv_b_in[i], conv_w_dw[i], conv_b_dw[i],
                            conv_ln_g[i], conv_ln_b[i], conv_w_out[i], conv_b_out[i])
        h = _peer_layer(h, norm_ffn[layer], peer_w_q[layer], peer_subkeys[layer], peer_u[layer],
                        peer_v[layer])
    return _final_norm(h.reshape(B * S, D), norm_final).reshape(B, S, D)
```

```python
import functools

import jax
import jax.numpy as jnp
import numpy as np
from jax import lax
from jax.experimental import pallas as pl
from jax.experimental.pallas import tpu as pltpu

F32 = jnp.float32
BF16 = jnp.bfloat16

D_MODEL = 1024
ROPE_THETA = 10000.0
NORM_EPS = 1e-6

MLA_HEADS = 8
MLA_Q_LORA = 384
MLA_KV_LORA = 256
MLA_NOPE = 64
MLA_ROPE = 32
MLA_V = 64

MOBA_HEADS = 8
MOBA_HEAD_DIM = 64
MOBA_BLOCK = 256
MOBA_TOPK = 3

CONV_WIDTH = 31

PEER_HEADS = 8
PEER_N_KEYS = 128
PEER_D_HALF = 128
PEER_TOPK = 16

LANES = 128
SUBLANES = 8
HEAD_PAIRS = 4
VMEM_LIMIT = 56 * 1024 * 1024
NEG = -1e30
LOG2E = 1.4426950408889634

ATTN_TQ = 512
ATTN_TK = 512
PROJ_TM = 256
PEER_ROUTE_TM = 256
PEER_TM = 512
PEER_A_TILE = 8
NOT_RANKED = 255.0
CONV_TS = 512
CONV_HALO = 32
CONV_ROWS = 32


def _params(*sem):
    return pltpu.CompilerParams(dimension_semantics=sem, vmem_limit_bytes=VMEM_LIMIT)


def _dot(a, b):
    return jnp.dot(a, b, preferred_element_type=F32)


def _dot_nt(a, b):
    return lax.dot_general(a, b, (((1,), (1,)), ((), ())), preferred_element_type=F32)


def _split(a):
    hi = a.astype(BF16)
    lo = (a - hi.astype(F32)).astype(BF16)
    return hi, lo


def _dot3(ah, al, bh, bl):
    return _dot(ah, bh) + (_dot(ah, bl) + _dot(al, bh))


def _dot3_nt(ah, al, bh, bl):
    return _dot_nt(ah, bh) + (_dot_nt(ah, bl) + _dot_nt(al, bh))


def _rms(x, g):
    return x * lax.rsqrt(jnp.mean(x * x, axis=-1, keepdims=True) + NORM_EPS) * g


def _rope_kernel(pos_ref, fr_ref, fm_ref, cr_ref, sr_ref, cm_ref, sm_ref):
    pos = pos_ref[0].astype(F32)
    ang_r = pos * fr_ref[0:1, :]
    ang_m = pos * fm_ref[0:1, :]
    cr_ref[0] = jnp.cos(ang_r) * fr_ref[1:2, :] + fr_ref[2:3, :]
    sr_ref[0] = jnp.sin(ang_r) * fr_ref[1:2, :]
    cm_ref[0] = jnp.cos(ang_m)
    sm_ref[0] = jnp.sin(ang_m)


def _rope_tables(positions):
    B, S = positions.shape
    ts = 512
    inv_r = 1.0 / (ROPE_THETA ** (jnp.arange(0, MLA_ROPE, 2, dtype=F32) / MLA_ROPE))
    inv_m = 1.0 / (ROPE_THETA ** (jnp.arange(0, MOBA_HEAD_DIM, 2, dtype=F32) / MOBA_HEAD_DIM))
    z32 = jnp.zeros((32,), F32)
    z64 = jnp.zeros((64,), F32)
    fr = jnp.stack([jnp.concatenate([z64, inv_r, inv_r, z32]),
                    jnp.concatenate([z64, jnp.ones((32,), F32), z32]),
                    jnp.concatenate([jnp.ones((64,), F32), z32, z32])])
    fr = jnp.concatenate([fr, jnp.zeros((5, LANES), F32)])
    fm = jnp.concatenate([jnp.tile(inv_m, 4)[None, :], jnp.zeros((7, LANES), F32)])
    tab = jax.ShapeDtypeStruct((B, S, LANES), F32)
    row = pl.BlockSpec((1, ts, LANES), lambda b, i: (b, i, 0))
    return pl.pallas_call(
        _rope_kernel,
        grid=(B, S // ts),
        in_specs=[pl.BlockSpec((1, ts, 1), lambda b, i: (b, i, 0)),
                  pl.BlockSpec((8, LANES), lambda b, i: (0, 0)),
                  pl.BlockSpec((8, LANES), lambda b, i: (0, 0))],
        out_specs=[row, row, row, row],
        out_shape=[tab, tab, tab, tab],
        compiler_params=_params("parallel", "parallel"),
        name="rope_tables",
    )(positions.reshape(B, S, 1), fr, fm)


def _attn_prep_kernel(h_ref, g_ref, w1_ref, w2h_ref, w2l_ref, gq_ref, gkv_ref, wq_ref, wqr_ref,
                      wk_ref, wv_ref, cr_ref, sr_ref, cm_ref, sm_ref,
                      qm_ref, km_ref, vm_ref, qb_ref, qf_ref, kf_ref, kb_ref, vb_ref):
    xn = _rms(h_ref[0], g_ref[...])
    xh, xl = _split(xn)
    p1 = _dot(xh, w1_ref[...])
    p2 = _dot3(xh, xl, w2h_ref[...], w2l_ref[...])
    cq, ckv = p1[:, 0:384], p1[:, 384:640]
    kr1, kr2 = p1[:, 640:768], p1[:, 768:896]
    mv = p1[:, 896:1408]
    cr, sr, cm, sm = cr_ref[0], sr_ref[0], cm_ref[0], sm_ref[0]
    nq = _rms(cq, gq_ref[...]).astype(BF16)
    nkv = _rms(ckv, gkv_ref[...]).astype(BF16)
    qa = _dot(nq, wq_ref[...])
    qr = _dot(nq, wqr_ref[...])
    ka = _dot(nkv, wk_ref[...])
    va = _dot(nkv, wv_ref[...])
    kpe = kr1 * cr + kr2 * sr
    mla_scale = (MLA_NOPE + MLA_ROPE) ** -0.5 * LOG2E
    for h in range(MLA_HEADS):
        sl = slice(h * LANES, (h + 1) * LANES)
        qm_ref[0, h] = ((qa[:, sl] * cr + qr[:, sl] * sr) * mla_scale).astype(BF16)
        km_ref[0, h] = (ka[:, sl] + kpe).astype(BF16)
        vm_ref[0, h] = va[:, sl].astype(BF16)
    low = lax.broadcasted_iota(jnp.int32, cm.shape, 1) < MOBA_HEAD_DIM
    moba_scale = MOBA_HEAD_DIM ** -0.5 * LOG2E
    for p in range(HEAD_PAIRS):
        sl = slice(p * LANES, (p + 1) * LANES)
        q = p2[:, sl] * cm + p2[:, 512 + p * LANES:512 + (p + 1) * LANES] * sm
        k = p2[:, 1024 + p * LANES:1024 + (p + 1) * LANES] * cm + p2[:, 1536 + p * LANES:1536 + (p + 1) * LANES] * sm
        v = mv[:, sl]
        qf_ref[0, :, sl] = q
        kf_ref[0, :, sl] = k
        qb_ref[0, :, sl] = (q * moba_scale).astype(BF16)
        kb_ref[0, 2 * p] = jnp.where(low, k, 0.0).astype(BF16)
        kb_ref[0, 2 * p + 1] = jnp.where(low, 0.0, k).astype(BF16)
        vb_ref[0, 2 * p] = jnp.where(low, v, 0.0).astype(BF16)
        vb_ref[0, 2 * p + 1] = jnp.where(low, 0.0, v).astype(BF16)


def _attn_prep(h, g, w, tabs):
    B, S, D = h.shape
    tm = PROJ_TM
    const = lambda a: pl.BlockSpec(a.shape, lambda b, i: (0,) * a.ndim)
    row = lambda n: pl.BlockSpec((1, tm, n), lambda b, i: (b, i, 0))
    head = pl.BlockSpec((1, 8, tm, LANES), lambda b, i: (b, 0, i, 0))
    head_shape = jax.ShapeDtypeStruct((B, 8, S, LANES), BF16)
    weights = [g, w["w1"], w["w2h"], w["w2l"], w["gq"], w["gkv"], w["wq"], w["wqr"], w["wk"], w["wv"]]
    return pl.pallas_call(
        _attn_prep_kernel,
        grid=(B, S // tm),
        in_specs=[row(D)] + [const(a) for a in weights] + [row(LANES)] * 4,
        out_specs=[head, head, head, row(512), row(512), row(512), head, head],
        out_shape=[head_shape, head_shape, head_shape,
                   jax.ShapeDtypeStruct((B, S, 512), BF16),
                   jax.ShapeDtypeStruct((B, S, 512), F32),
                   jax.ShapeDtypeStruct((B, S, 512), F32),
                   head_shape, head_shape],
        compiler_params=_params("parallel", "parallel"),
        name="attn_prep",
    )(h, *weights, *tabs)


def _moba_gate_kernel(q_ref, k_ref, sel_ref, kmh_ref, kml_ref):
    i = pl.program_id(2)
    tq = q_ref.shape[1]
    S = k_ref.shape[1]
    nb = S // MOBA_BLOCK

    @pl.when(i == 0)
    def _():
        km = jnp.sum(k_ref[0].reshape(nb, MOBA_BLOCK, LANES), axis=1) / float(MOBA_BLOCK)
        low = lax.broadcasted_iota(jnp.int32, km.shape, 1) < MOBA_HEAD_DIM
        for hh, m in enumerate((jnp.where(low, km, 0.0), jnp.where(low, 0.0, km))):
            hi, lo = _split(m)
            kmh_ref[hh] = hi
            kml_ref[hh] = lo

    qh, ql = _split(q_ref[0])
    row = i * tq + lax.broadcasted_iota(jnp.int32, (tq, nb), 0)
    blk = lax.broadcasted_iota(jnp.int32, (tq, nb), 1)
    valid = blk * MOBA_BLOCK < row - (row & (MOBA_BLOCK - 1))
    for hh in range(2):
        gate = _dot3_nt(qh, ql, kmh_ref[hh], kml_ref[hh])
        g = jnp.where(valid, gate, -jnp.inf)
        rest = g
        for _ in range(MOBA_TOPK - 1):
            top = jnp.max(rest, axis=1, keepdims=True)
            rest = jnp.where(rest == top, -jnp.inf, rest)
        thr = jnp.max(rest, axis=1, keepdims=True)
        sel_ref[0, hh] = jnp.where(valid & (g >= thr), 1.0, 0.0)


def _moba_gate(qf, kf):
    B, S, _ = qf.shape
    tq = ATTN_TQ
    nb = S // MOBA_BLOCK
    return pl.pallas_call(
        _moba_gate_kernel,
        grid=(B, HEAD_PAIRS, S // tq),
        in_specs=[pl.BlockSpec((1, tq, LANES), lambda b, p, i: (b, i, p)),
                  pl.BlockSpec((1, S, LANES), lambda b, p, i: (b, 0, p))],
        out_specs=pl.BlockSpec((1, 2, tq, nb), lambda b, p, i: (b, p, i, 0)),
        out_shape=jax.ShapeDtypeStruct((B, 8, S, nb), F32),
        scratch_shapes=[pltpu.VMEM((2, nb, LANES), BF16), pltpu.VMEM((2, nb, LANES), BF16)],
        compiler_params=_params("parallel", "parallel", "arbitrary"),
        name="moba_gate",
    )(qf, kf)


def _flash_kernel(*refs, moba, paired_q):
    if moba:
        q_ref, k_ref, v_ref, sel_ref, o_ref, m_scr, l_scr, acc_scr = refs
    else:
        q_ref, k_ref, v_ref, o_ref, m_scr, l_scr, acc_scr = refs
    i = pl.program_id(2)
    tq, tk = ATTN_TQ, ATTN_TK
    if paired_q:
        qs = (q_ref[0], q_ref[0])
    else:
        qs = (q_ref[0, 0], q_ref[0, 1])
    m_scr[...] = jnp.full(m_scr.shape, NEG, F32)
    l_scr[...] = jnp.zeros(l_scr.shape, F32)
    acc_scr[...] = jnp.zeros(acc_scr.shape, F32)
    low = lax.broadcasted_iota(jnp.int32, (tq, LANES), 1) < 64
    nsel = sel_ref.shape[3] if moba else 0

    def step(j, diag):
        start = pl.multiple_of(j * tk, tk)
        alphas, pvs = [], []
        for hh in range(2):
            k = k_ref[0, hh, pl.ds(start, tk), :]
            v = v_ref[0, hh, pl.ds(start, tk), :]
            s = _dot_nt(qs[hh], k)
            if moba:
                sel = sel_ref[0, hh]
                col = lax.broadcasted_iota(jnp.int32, sel.shape, 1)
                sub = tk // MOBA_BLOCK
                picked = jnp.concatenate(
                    [jnp.broadcast_to(
                        jnp.sum(jnp.where(col == j * sub + r, sel, 0.0), axis=1, keepdims=True),
                        (tq, MOBA_BLOCK)) for r in range(sub)], axis=1) > 0.5
            if diag:
                qpos = i * tq + lax.broadcasted_iota(jnp.int32, (tq, tk), 0)
                kpos = j * tk + lax.broadcasted_iota(jnp.int32, (tq, tk), 1)
                ok = kpos <= qpos
                if moba:
                    ok = (ok & (kpos >= qpos - (qpos & (MOBA_BLOCK - 1)))) | picked
                s = jnp.where(ok, s, NEG)
            elif moba:
                s = jnp.where(picked, s, NEG)
            m_prev = m_scr[hh]
            m_next = jnp.maximum(m_prev, jnp.max(s, axis=1, keepdims=True))
            alpha = jnp.exp2(m_prev - m_next)
            p = jnp.exp2(s - jnp.tile(m_next, (1, tk // LANES)))
            l_scr[hh] = alpha * l_scr[hh] + jnp.sum(p, axis=1, keepdims=True)
            m_scr[hh] = m_next
            alphas.append(alpha)
            pvs.append(_dot(p.astype(BF16), v))
        acc_scr[...] = acc_scr[...] * jnp.where(low, alphas[0], alphas[1]) + (pvs[0] + pvs[1])

    n_full = i * (tq // tk)

    def body(j, carry):
        step(j, False)
        return carry

    lax.fori_loop(0, n_full, body, 0)
    for r in range(tq // tk):
        step(n_full + r, True)
    o_ref[0] = (acc_scr[...] / jnp.where(low, l_scr[0], l_scr[1])).astype(o_ref.dtype)


def _flash(q, k, v, sel=None):
    moba = sel is not None
    B, _, S, _ = k.shape
    tq = ATTN_TQ
    kv_spec = pl.BlockSpec((1, 2, S, LANES), lambda b, p, i: (b, p, 0, 0))
    if moba:
        q_spec = pl.BlockSpec((1, tq, LANES), lambda b, p, i: (b, i, p))
        extra = [pl.BlockSpec((1, 2, tq, sel.shape[3]), lambda b, p, i: (b, p, i, 0))]
        args = (q, k, v, sel)
    else:
        q_spec = pl.BlockSpec((1, 2, tq, LANES), lambda b, p, i: (b, p, i, 0))
        extra = []
        args = (q, k, v)
    return pl.pallas_call(
        functools.partial(_flash_kernel, moba=moba, paired_q=moba),
        grid=(B, HEAD_PAIRS, S // tq),
        in_specs=[q_spec, kv_spec, kv_spec] + extra,
        out_specs=pl.BlockSpec((1, tq, LANES), lambda b, p, i: (b, i, p)),
        out_shape=jax.ShapeDtypeStruct((B, S, HEAD_PAIRS * LANES), BF16),
        scratch_shapes=[pltpu.VMEM((2, tq, LANES), F32), pltpu.VMEM((2, tq, LANES), F32),
                        pltpu.VMEM((tq, LANES), F32)],
        compiler_params=_params("parallel", "parallel", "arbitrary"),
        name="flash_moba" if moba else "flash_mla",
    )(*args)


def _attn_out_kernel(h_ref, oa_ref, ob_ref, w_ref, o_ref):
    half = oa_ref.shape[1]
    o_ref[...] = h_ref[...] + (_dot(oa_ref[...], w_ref[0:half, :]) + _dot(ob_ref[...], w_ref[half:, :]))


def _attn_out(h2, o_mla, o_moba, w_out):
    T, D = h2.shape
    tm = 512
    half = o_mla.shape[1]
    return pl.pallas_call(
        _attn_out_kernel,
        grid=(T // tm,),
        in_specs=[pl.BlockSpec((tm, D), lambda i: (i, 0)),
                  pl.BlockSpec((tm, half), lambda i: (i, 0)),
                  pl.BlockSpec((tm, half), lambda i: (i, 0)),
                  pl.BlockSpec(w_out.shape, lambda i: (0, 0))],
        out_specs=pl.BlockSpec((tm, D), lambda i: (i, 0)),
        out_shape=jax.ShapeDtypeStruct((T, D), F32),
        compiler_params=_params("parallel"),
        name="attn_out",
    )(h2, o_mla, o_moba, w_out)


def _attn_weights(w_in, q_norm, w_uq, kv_norm, w_ukv, w_out):
    D = w_in.shape[0]
    w_cq, w_ckv = w_in[:, 0:384], w_in[:, 384:640]
    w_kr = w_in[:, 640:672]
    w_mq, w_mk, w_mv = w_in[:, 672:1184], w_in[:, 1184:1696], w_in[:, 1696:2208]
    z = lambda n: jnp.zeros((D, n), F32)
    kr1 = jnp.concatenate([z(64), w_kr, z(32)], axis=1)
    kr2 = jnp.concatenate([z(64), -w_kr[:, 16:32], w_kr[:, 0:16], z(32)], axis=1)
    w1 = jnp.concatenate([w_cq, w_ckv, kr1, kr2, w_mv], axis=1).astype(BF16)

    def rot(w):
        w4 = w.reshape(D, MOBA_HEADS, 2, MOBA_HEAD_DIM // 2)
        return jnp.stack([-w4[:, :, 1], w4[:, :, 0]], axis=2).reshape(D, MOBA_HEADS * MOBA_HEAD_DIM)

    w2 = jnp.concatenate([w_mq, rot(w_mq), w_mk, rot(w_mk)], axis=1)
    w2h = w2.astype(BF16)
    w2l = (w2 - w2h.astype(F32)).astype(BF16)

    uq = w_uq.reshape(MLA_Q_LORA, MLA_HEADS, MLA_NOPE + MLA_ROPE)
    nope, pe1, pe2 = uq[..., :64], uq[..., 64:80], uq[..., 80:96]
    zq = lambda n: jnp.zeros((MLA_Q_LORA, MLA_HEADS, n), F32)
    wq = jnp.concatenate([nope, pe1, pe2, zq(32)], axis=-1).reshape(MLA_Q_LORA, 1024).astype(BF16)
    wqr = jnp.concatenate([zq(64), -pe2, pe1, zq(32)], axis=-1).reshape(MLA_Q_LORA, 1024).astype(BF16)

    ukv = w_ukv.reshape(MLA_KV_LORA, MLA_HEADS, MLA_NOPE + MLA_V)
    k_nope, v = ukv[..., :64], ukv[..., 64:]
    zk = jnp.zeros((MLA_KV_LORA, MLA_HEADS, 64), F32)
    wk = jnp.concatenate([k_nope, zk], axis=-1).reshape(MLA_KV_LORA, 1024).astype(BF16)
    even = (jnp.arange(MLA_HEADS) % 2 == 0)[None, :, None]
    wv = jnp.concatenate([jnp.where(even, v, 0.0), jnp.where(even, 0.0, v)], axis=-1)
    wv = wv.reshape(MLA_KV_LORA, 1024).astype(BF16)
    return dict(w1=w1, w2h=w2h, w2l=w2l, gq=q_norm[None, :], gkv=kv_norm[None, :], wq=wq, wqr=wqr,
                wk=wk, wv=wv, w_out=w_out.astype(BF16))


def _attention_layer(h, g, w, tabs):
    B, S, D = h.shape
    qm, km, vm, qb, qf, kf, kb, vb = _attn_prep(h, g[None, :], w, tabs)
    o_mla = _flash(qm, km, vm)
    sel = _moba_gate(qf, kf)
    o_moba = _flash(qb, kb, vb, sel)
    out = _attn_out(h.reshape(B * S, D), o_mla.reshape(B * S, -1), o_moba.reshape(B * S, -1), w["w_out"])
    return out.reshape(B, S, D)


def _conv_in_kernel(h_ref, g_ref, wu_ref, wg_ref, bu_ref, bg_ref, o_ref):
    xn = _rms(h_ref[...], g_ref[...]).astype(BF16)
    u = _dot(xn, wu_ref[...]) + bu_ref[...]
    gate = _dot(xn, wg_ref[...]) + bg_ref[...]
    o_ref[...] = u * jax.nn.sigmoid(gate)


def _conv_in(h2, g, w_in, b_in):
    T, D = h2.shape
    tm = 512
    const = lambda a: pl.BlockSpec(a.shape, lambda i: (0, 0))
    args = [g[None, :], w_in[:, :D].astype(BF16), w_in[:, D:].astype(BF16), b_in[None, :D], b_in[None, D:]]
    return pl.pallas_call(
        _conv_in_kernel,
        grid=(T // tm,),
        in_specs=[pl.BlockSpec((tm, D), lambda i: (i, 0))] + [const(a) for a in args],
        out_specs=pl.BlockSpec((tm, D), lambda i: (i, 0)),
        out_shape=jax.ShapeDtypeStruct((T, D), F32),
        compiler_params=_params("parallel"),
        name="conv_in",
    )(h2, *args)


def _conv_main_kernel(u_ref, up_ref, h_ref, wdw_ref, bdw_ref, lg_ref, lb_ref, wo_ref, bo_ref, o_ref,
                      buf_scr, y_scr):
    i = pl.program_id(1)
    ts = u_ref.shape[1]
    rows = CONV_HALO + ts
    buf_scr[0, 0:CONV_HALO, :] = jnp.where(i == 0, 0.0, up_ref[0])
    buf_scr[0, CONV_HALO:rows, :] = u_ref[0]
    for ph in range(1, SUBLANES):
        buf_scr[ph] = pltpu.roll(buf_scr[0], rows - ph, axis=0)
    lead = CONV_HALO - (CONV_WIDTH - 1)
    for r in range(ts // CONV_ROWS):
        acc = jnp.broadcast_to(bdw_ref[...], (CONV_ROWS, bdw_ref.shape[1]))
        for w in range(CONV_WIDTH):
            ph = (lead + w) % SUBLANES
            base = r * CONV_ROWS + (lead + w) - ph
            acc = acc + buf_scr[ph, base:base + CONV_ROWS, :] * jnp.tile(wdw_ref[w], (CONV_ROWS // SUBLANES, 1))
        mu = jnp.mean(acc, axis=-1, keepdims=True)
        cen = acc - mu
        var = jnp.mean(cen * cen, axis=-1, keepdims=True)
        y = cen * lax.rsqrt(var + NORM_EPS) * lg_ref[...] + lb_ref[...]
        y_scr[r * CONV_ROWS:(r + 1) * CONV_ROWS, :] = (y * jax.nn.sigmoid(y)).astype(BF16)
    o_ref[0] = h_ref[0] + (_dot(y_scr[...], wo_ref[...]) + bo_ref[...])


def _conv_main(u, h, w_dw, b_dw, ln_g, ln_b, w_out, b_out):
    B, S, D = h.shape
    ts = CONV_TS
    per = ts // CONV_HALO
    wdw = jnp.broadcast_to(w_dw[:, None, :], (CONV_WIDTH, SUBLANES, D))
    const = lambda a: pl.BlockSpec(a.shape, lambda b, i: (0,) * a.ndim)
    args = [wdw, b_dw[None, :], ln_g[None, :], ln_b[None, :], w_out.astype(BF16), b_out[None, :]]
    tile = pl.BlockSpec((1, ts, D), lambda b, i: (b, i, 0))
    return pl.pallas_call(
        _conv_main_kernel,
        grid=(B, S // ts),
        in_specs=[tile,
                  pl.BlockSpec((1, CONV_HALO, D), lambda b, i: (b, jnp.maximum(i * per - 1, 0), 0)),
                  tile] + [const(a) for a in args],
        out_specs=tile,
        out_shape=jax.ShapeDtypeStruct((B, S, D), F32),
        scratch_shapes=[pltpu.VMEM((SUBLANES, CONV_HALO + ts, D), F32), pltpu.VMEM((ts, D), BF16)],
        compiler_params=_params("parallel", "arbitrary"),
        name="conv_main",
    )(u, u, h, *args)


def _conv_layer(h, g, w_in, b_in, w_dw, b_dw, ln_g, ln_b, w_out, b_out):
    B, S, D = h.shape
    u = _conv_in(h.reshape(B * S, D), g, w_in, b_in).reshape(B, S, D)
    return _conv_main(u, h, w_dw, b_dw, ln_g, ln_b, w_out, b_out)


def _col_max(x):
    return jnp.max(x, axis=0, keepdims=True)


def _dup_bf16(x):
    bits = pltpu.bitcast(x.astype(BF16).astype(F32), jnp.uint32)
    return (bits & jnp.uint32(0xFFFF0000)) | (bits >> 16)


def _peer_route_kernel(h_ref, g_ref, wqh_ref, wql_ref, skh_ref, skl_ref,
                       xn_ref, cnt_ref, e1_ref, rank_ref, e2_ref, q_scr, top_scr):
    xn = _rms(h_ref[...], g_ref[...])
    xh, xl = _split(xn)
    xn_ref[...] = xh
    q = _dot3(xh, xl, wqh_ref[...], wql_ref[...])
    for hc in range(2 * PEER_HEADS):
        q_scr[hc] = q[:, hc * PEER_D_HALF:(hc + 1) * PEER_D_HALF]

    def head(h, carry):
        s = []
        rank = None
        for c in range(2):
            qh, ql = _split(q_scr[2 * h + c])
            sc = _dot3_nt(skh_ref[2 * h + c], skl_ref[2 * h + c], qh, ql)
            s.append(sc)
            rest = sc
            rank = jnp.full(sc.shape, NOT_RANKED, F32)
            for k in range(PEER_TOPK):
                top = _col_max(rest)
                top_scr[c, k:k + 1, :] = top
                hit = rest == top
                if c == 1:
                    rank = jnp.where(hit, float(k), rank)
                if k + 1 < PEER_TOPK:
                    rest = jnp.where(hit, -jnp.inf, rest)
        a = top_scr[0]
        b = top_scr[1]
        cands = [a[0:1] + b, a[1:2] + b[0:8]]
        cands += [a[i:i + 1] + b[0:8] for i in range(2, 8)]
        cands += [a[8:16] + b[0:1]]
        m_top = a[0:1] + b[0:1]
        z = jnp.zeros_like(m_top)
        thr = m_top
        for k in range(PEER_TOPK):
            part = jnp.maximum(cands[0][0:8], cands[0][8:16])
            for cnd in cands[1:]:
                part = jnp.maximum(part, cnd)
            thr = _col_max(part)
            z = z + jnp.exp(thr - m_top)
            if k + 1 < PEER_TOPK:
                cands = [jnp.where(cnd == thr, -jnp.inf, cnd) for cnd in cands]
        cnt = jnp.zeros(s[0].shape, F32)
        for j in range(PEER_TOPK):
            cnt = cnt + jnp.where(s[0] + b[j:j + 1] >= thr, 1.0, 0.0)
        cnt_ref[h] = _dup_bf16(cnt)
        e1_ref[h] = _dup_bf16(jnp.exp(s[0] - a[0:1]) * (0.5 / z))
        rank_ref[h] = rank.astype(BF16)
        e2_ref[h] = jnp.exp(s[1] - b[0:1]).astype(BF16)
        return carry

    lax.fori_loop(0, PEER_HEADS, head, 0)


def _peer_route(h2, g, wqh, wql, skh, skl):
    T, D = h2.shape
    tm = PEER_ROUTE_TM
    const = lambda a: pl.BlockSpec(a.shape, lambda i: (0,) * a.ndim)
    stat = pl.BlockSpec((PEER_HEADS, PEER_N_KEYS, tm), lambda i: (0, 0, i))
    a_shape = jax.ShapeDtypeStruct((PEER_HEADS, PEER_N_KEYS, T), jnp.uint32)
    b_shape = jax.ShapeDtypeStruct((PEER_HEADS, PEER_N_KEYS, T), BF16)
    args = [g[None, :], wqh, wql, skh, skl]
    return pl.pallas_call(
        _peer_route_kernel,
        grid=(T // tm,),
        in_specs=[pl.BlockSpec((tm, D), lambda i: (i, 0))] + [const(a) for a in args],
        out_specs=[pl.BlockSpec((tm, D), lambda i: (i, 0)), stat, stat, stat, stat],
        out_shape=[jax.ShapeDtypeStruct((T, D), BF16), a_shape, a_shape, b_shape, b_shape],
        scratch_shapes=[pltpu.VMEM((2 * PEER_HEADS, tm, PEER_D_HALF), F32),
                        pltpu.VMEM((2, PEER_TOPK, tm), F32)],
        compiler_params=_params("parallel"),
        name="peer_route",
    )(h2, *args)


def _gelu_x2(x):
    return x + x * lax.erf(x * (2.0 ** -0.5))


def _bcast_row(ref, h, a):
    row = ref[h, a:a + 1, :]
    return pltpu.bitcast(jnp.broadcast_to(row, (8, row.shape[1])), BF16)


def _peer_main_kernel(xn_ref, u_ref, vt_ref, cnt_ref, e1_ref, rank_ref, e2_ref, h_ref, o_ref, acc_scr, w_scr):
    k = pl.program_id(1)
    tm = w_scr.shape[1]
    pair = 2 * PEER_N_KEYS

    @pl.when(k == 0)
    def _():
        acc_scr[...] = jnp.zeros(acc_scr.shape, F32)

    for p in range(PEER_A_TILE // 2):
        prow = slice(p * pair, (p + 1) * pair)
        st = _dot_nt(u_ref[prow, :], xn_ref[...])
        for r in range(2):
            a = 2 * p + r
            gsum = jnp.zeros((PEER_N_KEYS, tm), BF16)
            for h in range(PEER_HEADS):
                cnt = _bcast_row(cnt_ref, h, a)
                e1 = _bcast_row(e1_ref, h, a)
                reps = (PEER_N_KEYS // cnt.shape[0], 1)
                picked = rank_ref[h] < jnp.tile(cnt, reps)
                gsum = gsum + jnp.where(picked, e2_ref[h] * jnp.tile(e1, reps), jnp.zeros((), BF16))
            act = _gelu_x2(st[r * PEER_N_KEYS:(r + 1) * PEER_N_KEYS, :]).astype(BF16)
            w_scr[a * PEER_N_KEYS:(a + 1) * PEER_N_KEYS, :] = gsum * act
        acc_scr[...] += _dot(vt_ref[:, prow], w_scr[prow, :])

    @pl.when(k == pl.num_programs(1) - 1)
    def _():
        o_ref[...] = h_ref[...] + acc_scr[...].T


def _peer_main(h2, xn, u_bf, vt_bf, cnt, e1, rank, e2):
    T, D = h2.shape
    E = u_bf.shape[0]
    tm = PEER_TM
    te = PEER_A_TILE * PEER_N_KEYS
    tok = pl.BlockSpec((tm, D), lambda i, k: (i, 0))
    a_stat = pl.BlockSpec((PEER_HEADS, PEER_A_TILE, tm), lambda i, k: (0, k, i))
    b_stat = pl.BlockSpec((PEER_HEADS, PEER_N_KEYS, tm), lambda i, k: (0, 0, i))
    return pl.pallas_call(
        _peer_main_kernel,
        grid=(T // tm, E // te),
        in_specs=[tok,
                  pl.BlockSpec((te, D), lambda i, k: (k, 0)),
                  pl.BlockSpec((D, te), lambda i, k: (0, k)),
                  a_stat, a_stat, b_stat, b_stat, tok],
        out_specs=tok,
        out_shape=jax.ShapeDtypeStruct((T, D), F32),
        scratch_shapes=[pltpu.VMEM((D, tm), F32), pltpu.VMEM((te, tm), BF16)],
        compiler_params=_params("parallel", "arbitrary"),
        name="peer_main",
    )(xn, u_bf, vt_bf, cnt, e1, rank, e2, h2)


def _peer_layer(h, g, w_q, subkeys, u_tab, v_tab):
    B, S, D = h.shape
    h2 = h.reshape(B * S, D)
    wqh = w_q.astype(BF16)
    wql = (w_q - wqh.astype(F32)).astype(BF16)
    sk = subkeys.reshape(2 * PEER_HEADS, PEER_N_KEYS, PEER_D_HALF)
    skh = sk.astype(BF16)
    skl = (sk - skh.astype(F32)).astype(BF16)
    xn, cnt, e1, rank, e2 = _peer_route(h2, g, wqh, wql, skh, skl)
    out = _peer_main(h2, xn, u_tab.astype(BF16), v_tab.T.astype(BF16), cnt, e1, rank, e2)
    return out.reshape(B, S, D)


def _final_norm_kernel(h_ref, g_ref, o_ref):
    o_ref[...] = _rms(h_ref[...], g_ref[...])


def _final_norm(h2, g):
    T, D = h2.shape
    tm = 512
    return pl.pallas_call(
        _final_norm_kernel,
        grid=(T // tm,),
        in_specs=[pl.BlockSpec((tm, D), lambda i: (i, 0)), pl.BlockSpec((1, D), lambda i: (0, 0))],
        out_specs=pl.BlockSpec((tm, D), lambda i: (i, 0)),
        out_shape=jax.ShapeDtypeStruct((T, D), F32),
        compiler_params=_params("parallel"),
        name="final_norm",
    )(h2, g[None, :])


def kernel(x, positions, norm_mix, norm_ffn, norm_final, attn_w_in, mla_q_norm, mla_w_uq, mla_kv_norm,
           mla_w_ukv, attn_w_out, conv_w_in, conv_b_in, conv_w_dw, conv_b_dw, conv_ln_g, conv_ln_b,
           conv_w_out, conv_b_out, peer_w_q, peer_subkeys, peer_u, peer_v):
    B, S, D = x.shape
    assert D == D_MODEL and S % ATTN_TQ == 0 and S % MOBA_BLOCK == 0 and (B * S) % PEER_TM == 0
    depth = norm_mix.shape[0]
    tabs = _rope_tables(positions)
    h = x
    for layer in range(depth):
        i = layer // 2
        if layer % 2 == 0:
            w = _attn_weights(attn_w_in[i], mla_q_norm[i], mla_w_uq[i], mla_kv_norm[i], mla_w_ukv[i],
                              attn_w_out[i])
            h = _attention_layer(h, norm_mix[layer], w, tabs)
        else:
            h = _conv_layer(h, norm_mix[layer], conv_w_in[i], conv_b_in[i], conv_w_dw[i], conv_b_dw[i],
                            conv_ln_g[i], conv_ln_b[i], conv_w_out[i], conv_b_out[i])
        h = _peer_layer(h, norm_ffn[layer], peer_w_q[layer], peer_subkeys[layer], peer_u[layer],
                        peer_v[layer])
    return _final_norm(h.reshape(B * S, D), norm_final).reshape(B, S, D)
```

```python
import functools

import jax
import jax.numpy as jnp
import numpy as np
from jax import lax
from jax.experimental import pallas as pl
from jax.experimental.pallas import tpu as pltpu

F32 = jnp.float32
BF16 = jnp.bfloat16

D_MODEL = 1024
ROPE_THETA = 10000.0
NORM_EPS = 1e-6

MLA_HEADS = 8
MLA_Q_LORA = 384
MLA_KV_LORA = 256
MLA_NOPE = 64
MLA_ROPE = 32
MLA_V = 64

MOBA_HEADS = 8
MOBA_HEAD_DIM = 64
MOBA_BLOCK = 256
MOBA_TOPK = 3

CONV_WIDTH = 31

PEER_HEADS = 8
PEER_N_KEYS = 128
PEER_D_HALF = 128
PEER_TOPK = 16

LANES = 128
SUBLANES = 8
HEAD_PAIRS = 4
VMEM_LIMIT = 56 * 1024 * 1024
NEG = -1e30
LOG2E = 1.4426950408889634

ATTN_TQ = 512
ATTN_TK = 512
PROJ_TM = 256
PEER_ROUTE_TM = 256
PEER_TM = 512
PEER_A_TILE = 32
NOT_RANKED = 255.0
CONV_TS = 512
CONV_HALO = 32
CONV_ROWS = 32


def _params(*sem):
    return pltpu.CompilerParams(dimension_semantics=sem, vmem_limit_bytes=VMEM_LIMIT)


def _dot(a, b):
    return jnp.dot(a, b, preferred_element_type=F32)


def _dot_nt(a, b):
    return lax.dot_general(a, b, (((1,), (1,)), ((), ())), preferred_element_type=F32)


def _split(a):
    hi = a.astype(BF16)
    lo = (a - hi.astype(F32)).astype(BF16)
    return hi, lo


def _dot3(ah, al, bh, bl):
    return _dot(ah, bh) + (_dot(ah, bl) + _dot(al, bh))


def _dot3_nt(ah, al, bh, bl):
    return _dot_nt(ah, bh) + (_dot_nt(ah, bl) + _dot_nt(al, bh))


def _rms(x, g):
    return x * lax.rsqrt(jnp.mean(x * x, axis=-1, keepdims=True) + NORM_EPS) * g


def _rope_kernel(pos_ref, fr_ref, fm_ref, cr_ref, sr_ref, cm_ref, sm_ref):
    pos = pos_ref[0].astype(F32)
    ang_r = pos * fr_ref[0:1, :]
    ang_m = pos * fm_ref[0:1, :]
    cr_ref[0] = jnp.cos(ang_r) * fr_ref[1:2, :] + fr_ref[2:3, :]
    sr_ref[0] = jnp.sin(ang_r) * fr_ref[1:2, :]
    cm_ref[0] = jnp.cos(ang_m)
    sm_ref[0] = jnp.sin(ang_m)


def _rope_tables(positions):
    B, S = positions.shape
    ts = 512
    inv_r = 1.0 / (ROPE_THETA ** (jnp.arange(0, MLA_ROPE, 2, dtype=F32) / MLA_ROPE))
    inv_m = 1.0 / (ROPE_THETA ** (jnp.arange(0, MOBA_HEAD_DIM, 2, dtype=F32) / MOBA_HEAD_DIM))
    z32 = jnp.zeros((32,), F32)
    z64 = jnp.zeros((64,), F32)
    fr = jnp.stack([jnp.concatenate([z64, inv_r, inv_r, z32]),
                    jnp.concatenate([z64, jnp.ones((32,), F32), z32]),
                    jnp.concatenate([jnp.ones((64,), F32), z32, z32])])
    fr = jnp.concatenate([fr, jnp.zeros((5, LANES), F32)])
    fm = jnp.concatenate([jnp.tile(inv_m, 4)[None, :], jnp.zeros((7, LANES), F32)])
    tab = jax.ShapeDtypeStruct((B, S, LANES), F32)
    row = pl.BlockSpec((1, ts, LANES), lambda b, i: (b, i, 0))
    return pl.pallas_call(
        _rope_kernel,
        grid=(B, S // ts),
        in_specs=[pl.BlockSpec((1, ts, 1), lambda b, i: (b, i, 0)),
                  pl.BlockSpec((8, LANES), lambda b, i: (0, 0)),
                  pl.BlockSpec((8, LANES), lambda b, i: (0, 0))],
        out_specs=[row, row, row, row],
        out_shape=[tab, tab, tab, tab],
        compiler_params=_params("parallel", "parallel"),
        name="rope_tables",
    )(positions.reshape(B, S, 1), fr, fm)


def _attn_prep_kernel(h_ref, g_ref, w1_ref, w2h_ref, w2l_ref, gq_ref, gkv_ref, wq_ref, wqr_ref,
                      wk_ref, wv_ref, cr_ref, sr_ref, cm_ref, sm_ref,
                      qm_ref, km_ref, vm_ref, qb_ref, qf_ref, kf_ref, kb_ref, vb_ref):
    xn = _rms(h_ref[0], g_ref[...])
    xh, xl = _split(xn)
    p1 = _dot(xh, w1_ref[...])
    p2 = _dot3(xh, xl, w2h_ref[...], w2l_ref[...])
    cq, ckv = p1[:, 0:384], p1[:, 384:640]
    kr1, kr2 = p1[:, 640:768], p1[:, 768:896]
    mv = p1[:, 896:1408]
    cr, sr, cm, sm = cr_ref[0], sr_ref[0], cm_ref[0], sm_ref[0]
    nq = _rms(cq, gq_ref[...]).astype(BF16)
    nkv = _rms(ckv, gkv_ref[...]).astype(BF16)
    qa = _dot(nq, wq_ref[...])
    qr = _dot(nq, wqr_ref[...])
    ka = _dot(nkv, wk_ref[...])
    va = _dot(nkv, wv_ref[...])
    kpe = kr1 * cr + kr2 * sr
    mla_scale = (MLA_NOPE + MLA_ROPE) ** -0.5 * LOG2E
    for h in range(MLA_HEADS):
        sl = slice(h * LANES, (h + 1) * LANES)
        qm_ref[0, h] = ((qa[:, sl] * cr + qr[:, sl] * sr) * mla_scale).astype(BF16)
        km_ref[0, h] = (ka[:, sl] + kpe).astype(BF16)
        vm_ref[0, h] = va[:, sl].astype(BF16)
    low = lax.broadcasted_iota(jnp.int32, cm.shape, 1) < MOBA_HEAD_DIM
    moba_scale = MOBA_HEAD_DIM ** -0.5 * LOG2E
    for p in range(HEAD_PAIRS):
        sl = slice(p * LANES, (p + 1) * LANES)
        q = p2[:, sl] * cm + p2[:, 512 + p * LANES:512 + (p + 1) * LANES] * sm
        k = p2[:, 1024 + p * LANES:1024 + (p + 1) * LANES] * cm + p2[:, 1536 + p * LANES:1536 + (p + 1) * LANES] * sm
        v = mv[:, sl]
        qf_ref[0, :, sl] = q
        kf_ref[0, :, sl] = k
        qb_ref[0, :, sl] = (q * moba_scale).astype(BF16)
        kb_ref[0, 2 * p] = jnp.where(low, k, 0.0).astype(BF16)
        kb_ref[0, 2 * p + 1] = jnp.where(low, 0.0, k).astype(BF16)
        vb_ref[0, 2 * p] = jnp.where(low, v, 0.0).astype(BF16)
        vb_ref[0, 2 * p + 1] = jnp.where(low, 0.0, v).astype(BF16)


def _attn_prep(h, g, w, tabs):
    B, S, D = h.shape
    tm = PROJ_TM
    const = lambda a: pl.BlockSpec(a.shape, lambda b, i: (0,) * a.ndim)
    row = lambda n: pl.BlockSpec((1, tm, n), lambda b, i: (b, i, 0))
    head = pl.BlockSpec((1, 8, tm, LANES), lambda b, i: (b, 0, i, 0))
    head_shape = jax.ShapeDtypeStruct((B, 8, S, LANES), BF16)
    weights = [g, w["w1"], w["w2h"], w["w2l"], w["gq"], w["gkv"], w["wq"], w["wqr"], w["wk"], w["wv"]]
    return pl.pallas_call(
        _attn_prep_kernel,
        grid=(B, S // tm),
        in_specs=[row(D)] + [const(a) for a in weights] + [row(LANES)] * 4,
        out_specs=[head, head, head, row(512), row(512), row(512), head, head],
        out_shape=[head_shape, head_shape, head_shape,
                   jax.ShapeDtypeStruct((B, S, 512), BF16),
                   jax.ShapeDtypeStruct((B, S, 512), F32),
                   jax.ShapeDtypeStruct((B, S, 512), F32),
                   head_shape, head_shape],
        compiler_params=_params("parallel", "parallel"),
        name="attn_prep",
    )(h, *weights, *tabs)


def _moba_gate_kernel(q_ref, k_ref, sel_ref, kmh_ref, kml_ref):
    i = pl.program_id(2)
    tq = q_ref.shape[1]
    S = k_ref.shape[1]
    nb = S // MOBA_BLOCK

    @pl.when(i == 0)
    def _():
        km = jnp.sum(k_ref[0].reshape(nb, MOBA_BLOCK, LANES), axis=1) / float(MOBA_BLOCK)
        low = lax.broadcasted_iota(jnp.int32, km.shape, 1) < MOBA_HEAD_DIM
        for hh, m in enumerate((jnp.where(low, km, 0.0), jnp.where(low, 0.0, km))):
            hi, lo = _split(m)
            kmh_ref[hh] = hi
            kml_ref[hh] = lo

    qh, ql = _split(q_ref[0])
    row = i * tq + lax.broadcasted_iota(jnp.int32, (tq, nb), 0)
    blk = lax.broadcasted_iota(jnp.int32, (tq, nb), 1)
    valid = blk * MOBA_BLOCK < row - (row & (MOBA_BLOCK - 1))
    for hh in range(2):
        gate = _dot3_nt(qh, ql, kmh_ref[hh], kml_ref[hh])
        g = jnp.where(valid, gate, -jnp.inf)
        rest = g
        for _ in range(MOBA_TOPK - 1):
            top = jnp.max(rest, axis=1, keepdims=True)
            rest = jnp.where(rest == top, -jnp.inf, rest)
        thr = jnp.max(rest, axis=1, keepdims=True)
        sel_ref[0, hh] = jnp.where(valid & (g >= thr), 1.0, 0.0)


def _moba_gate(qf, kf):
    B, S, _ = qf.shape
    tq = ATTN_TQ
    nb = S // MOBA_BLOCK
    return pl.pallas_call(
        _moba_gate_kernel,
        grid=(B, HEAD_PAIRS, S // tq),
        in_specs=[pl.BlockSpec((1, tq, LANES), lambda b, p, i: (b, i, p)),
                  pl.BlockSpec((1, S, LANES), lambda b, p, i: (b, 0, p))],
        out_specs=pl.BlockSpec((1, 2, tq, nb), lambda b, p, i: (b, p, i, 0)),
        out_shape=jax.ShapeDtypeStruct((B, 8, S, nb), F32),
        scratch_shapes=[pltpu.VMEM((2, nb, LANES), BF16), pltpu.VMEM((2, nb, LANES), BF16)],
        compiler_params=_params("parallel", "parallel", "arbitrary"),
        name="moba_gate",
    )(qf, kf)


def _flash_kernel(*refs, moba, paired_q):
    if moba:
        q_ref, k_ref, v_ref, sel_ref, o_ref, m_scr, l_scr, acc_scr = refs
    else:
        q_ref, k_ref, v_ref, o_ref, m_scr, l_scr, acc_scr = refs
    i = pl.program_id(2)
    tq, tk = ATTN_TQ, ATTN_TK
    if paired_q:
        qs = (q_ref[0], q_ref[0])
    else:
        qs = (q_ref[0, 0], q_ref[0, 1])
    m_scr[...] = jnp.full(m_scr.shape, NEG, F32)
    l_scr[...] = jnp.zeros(l_scr.shape, F32)
    acc_scr[...] = jnp.zeros(acc_scr.shape, F32)
    low = lax.broadcasted_iota(jnp.int32, (tq, LANES), 1) < 64
    nsel = sel_ref.shape[3] if moba else 0

    def step(j, diag):
        start = pl.multiple_of(j * tk, tk)
        alphas, pvs = [], []
        for hh in range(2):
            k = k_ref[0, hh, pl.ds(start, tk), :]
            v = v_ref[0, hh, pl.ds(start, tk), :]
            s = _dot_nt(qs[hh], k)
            if moba:
                sel = sel_ref[0, hh]
                col = lax.broadcasted_iota(jnp.int32, sel.shape, 1)
                sub = tk // MOBA_BLOCK
                picked = jnp.concatenate(
                    [jnp.broadcast_to(
                        jnp.sum(jnp.where(col == j * sub + r, sel, 0.0), axis=1, keepdims=True),
                        (tq, MOBA_BLOCK)) for r in range(sub)], axis=1) > 0.5
            if diag:
                qpos = i * tq + lax.broadcasted_iota(jnp.int32, (tq, tk), 0)
                kpos = j * tk + lax.broadcasted_iota(jnp.int32, (tq, tk), 1)
                ok = kpos <= qpos
                if moba:
                    ok = (ok & (kpos >= qpos - (qpos & (MOBA_BLOCK - 1)))) | picked
                s = jnp.where(ok, s, NEG)
            elif moba:
                s = jnp.where(picked, s, NEG)
            m_prev = m_scr[hh]
            m_next = jnp.maximum(m_prev, jnp.max(s, axis=1, keepdims=True))
            alpha = jnp.exp2(m_prev - m_next)
            p = jnp.exp2(s - jnp.tile(m_next, (1, tk // LANES)))
            l_scr[hh] = alpha * l_scr[hh] + jnp.sum(p, axis=1, keepdims=True)
            m_scr[hh] = m_next
            alphas.append(alpha)
            pvs.append(_dot(p.astype(BF16), v))
        acc_scr[...] = acc_scr[...] * jnp.where(low, alphas[0], alphas[1]) + (pvs[0] + pvs[1])

    n_full = i * (tq // tk)

    def body(j, carry):
        step(j, False)
        return carry

    lax.fori_loop(0, n_full, body, 0)
    for r in range(tq // tk):
        step(n_full + r, True)
    o_ref[0] = (acc_scr[...] / jnp.where(low, l_scr[0], l_scr[1])).astype(o_ref.dtype)


def _flash(q, k, v, sel=None):
    moba = sel is not None
    B, _, S, _ = k.shape
    tq = ATTN_TQ
    kv_spec = pl.BlockSpec((1, 2, S, LANES), lambda b, p, i: (b, p, 0, 0))
    if moba:
        q_spec = pl.BlockSpec((1, tq, LANES), lambda b, p, i: (b, i, p))
        extra = [pl.BlockSpec((1, 2, tq, sel.shape[3]), lambda b, p, i: (b, p, i, 0))]
        args = (q, k, v, sel)
    else:
        q_spec = pl.BlockSpec((1, 2, tq, LANES), lambda b, p, i: (b, p, i, 0))
        extra = []
        args = (q, k, v)
    return pl.pallas_call(
        functools.partial(_flash_kernel, moba=moba, paired_q=moba),
        grid=(B, HEAD_PAIRS, S // tq),
        in_specs=[q_spec, kv_spec, kv_spec] + extra,
        out_specs=pl.BlockSpec((1, tq, LANES), lambda b, p, i: (b, i, p)),
        out_shape=jax.ShapeDtypeStruct((B, S, HEAD_PAIRS * LANES), BF16),
        scratch_shapes=[pltpu.VMEM((2, tq, LANES), F32), pltpu.VMEM((2, tq, LANES), F32),
                        pltpu.VMEM((tq, LANES), F32)],
        compiler_params=_params("parallel", "parallel", "arbitrary"),
        name="flash_moba" if moba else "flash_mla",
    )(*args)


def _attn_out_kernel(h_ref, oa_ref, ob_ref, w_ref, o_ref):
    half = oa_ref.shape[1]
    o_ref[...] = h_ref[...] + (_dot(oa_ref[...], w_ref[0:half, :]) + _dot(ob_ref[...], w_ref[half:, :]))


def _attn_out(h2, o_mla, o_moba, w_out):
    T, D = h2.shape
    tm = 512
    half = o_mla.shape[1]
    return pl.pallas_call(
        _attn_out_kernel,
        grid=(T // tm,),
        in_specs=[pl.BlockSpec((tm, D), lambda i: (i, 0)),
                  pl.BlockSpec((tm, half), lambda i: (i, 0)),
                  pl.BlockSpec((tm, half), lambda i: (i, 0)),
                  pl.BlockSpec(w_out.shape, lambda i: (0, 0))],
        out_specs=pl.BlockSpec((tm, D), lambda i: (i, 0)),
        out_shape=jax.ShapeDtypeStruct((T, D), F32),
        compiler_params=_params("parallel"),
        name="attn_out",
    )(h2, o_mla, o_moba, w_out)


def _attn_weights(w_in, q_norm, w_uq, kv_norm, w_ukv, w_out):
    D = w_in.shape[0]
    w_cq, w_ckv = w_in[:, 0:384], w_in[:, 384:640]
    w_kr = w_in[:, 640:672]
    w_mq, w_mk, w_mv = w_in[:, 672:1184], w_in[:, 1184:1696], w_in[:, 1696:2208]
    z = lambda n: jnp.zeros((D, n), F32)
    kr1 = jnp.concatenate([z(64), w_kr, z(32)], axis=1)
    kr2 = jnp.concatenate([z(64), -w_kr[:, 16:32], w_kr[:, 0:16], z(32)], axis=1)
    w1 = jnp.concatenate([w_cq, w_ckv, kr1, kr2, w_mv], axis=1).astype(BF16)

    def rot(w):
        w4 = w.reshape(D, MOBA_HEADS, 2, MOBA_HEAD_DIM // 2)
        return jnp.stack([-w4[:, :, 1], w4[:, :, 0]], axis=2).reshape(D, MOBA_HEADS * MOBA_HEAD_DIM)

    w2 = jnp.concatenate([w_mq, rot(w_mq), w_mk, rot(w_mk)], axis=1)
    w2h = w2.astype(BF16)
    w2l = (w2 - w2h.astype(F32)).astype(BF16)

    uq = w_uq.reshape(MLA_Q_LORA, MLA_HEADS, MLA_NOPE + MLA_ROPE)
    nope, pe1, pe2 = uq[..., :64], uq[..., 64:80], uq[..., 80:96]
    zq = lambda n: jnp.zeros((MLA_Q_LORA, MLA_HEADS, n), F32)
    wq = jnp.concatenate([nope, pe1, pe2, zq(32)], axis=-1).reshape(MLA_Q_LORA, 1024).astype(BF16)
    wqr = jnp.concatenate([zq(64), -pe2, pe1, zq(32)], axis=-1).reshape(MLA_Q_LORA, 1024).astype(BF16)

    ukv = w_ukv.reshape(MLA_KV_LORA, MLA_HEADS, MLA_NOPE + MLA_V)
    k_nope, v = ukv[..., :64], ukv[..., 64:]
    zk = jnp.zeros((MLA_KV_LORA, MLA_HEADS, 64), F32)
    wk = jnp.concatenate([k_nope, zk], axis=-1).reshape(MLA_KV_LORA, 1024).astype(BF16)
    even = (jnp.arange(MLA_HEADS) % 2 == 0)[None, :, None]
    wv = jnp.concatenate([jnp.where(even, v, 0.0), jnp.where(even, 0.0, v)], axis=-1)
    wv = wv.reshape(MLA_KV_LORA, 1024).astype(BF16)
    return dict(w1=w1, w2h=w2h, w2l=w2l, gq=q_norm[None, :], gkv=kv_norm[None, :], wq=wq, wqr=wqr,
                wk=wk, wv=wv, w_out=w_out.astype(BF16))


def _attention_layer(h, g, w, tabs):
    B, S, D = h.shape
    qm, km, vm, qb, qf, kf, kb, vb = _attn_prep(h, g[None, :], w, tabs)
    o_mla = _flash(qm, km, vm)
    sel = _moba_gate(qf, kf)
    o_moba = _flash(qb, kb, vb, sel)
    out = _attn_out(h.reshape(B * S, D), o_mla.reshape(B * S, -1), o_moba.reshape(B * S, -1), w["w_out"])
    return out.reshape(B, S, D)


def _conv_in_kernel(h_ref, g_ref, wu_ref, wg_ref, bu_ref, bg_ref, o_ref):
    xn = _rms(h_ref[...], g_ref[...]).astype(BF16)
    u = _dot(xn, wu_ref[...]) + bu_ref[...]
    gate = _dot(xn, wg_ref[...]) + bg_ref[...]
    o_ref[...] = u * jax.nn.sigmoid(gate)


def _conv_in(h2, g, w_in, b_in):
    T, D = h2.shape
    tm = 512
    const = lambda a: pl.BlockSpec(a.shape, lambda i: (0, 0))
    args = [g[None, :], w_in[:, :D].astype(BF16), w_in[:, D:].astype(BF16), b_in[None, :D], b_in[None, D:]]
    return pl.pallas_call(
        _conv_in_kernel,
        grid=(T // tm,),
        in_specs=[pl.BlockSpec((tm, D), lambda i: (i, 0))] + [const(a) for a in args],
        out_specs=pl.BlockSpec((tm, D), lambda i: (i, 0)),
        out_shape=jax.ShapeDtypeStruct((T, D), F32),
        compiler_params=_params("parallel"),
        name="conv_in",
    )(h2, *args)


def _conv_main_kernel(u_ref, up_ref, h_ref, wdw_ref, bdw_ref, lg_ref, lb_ref, wo_ref, bo_ref, o_ref,
                      buf_scr, y_scr):
    i = pl.program_id(1)
    ts = u_ref.shape[1]
    rows = CONV_HALO + ts
    buf_scr[0, 0:CONV_HALO, :] = jnp.where(i == 0, 0.0, up_ref[0])
    buf_scr[0, CONV_HALO:rows, :] = u_ref[0]
    for ph in range(1, SUBLANES):
        buf_scr[ph] = pltpu.roll(buf_scr[0], rows - ph, axis=0)
    lead = CONV_HALO - (CONV_WIDTH - 1)
    for r in range(ts // CONV_ROWS):
        acc = jnp.broadcast_to(bdw_ref[...], (CONV_ROWS, bdw_ref.shape[1]))
        for w in range(CONV_WIDTH):
            ph = (lead + w) % SUBLANES
            base = r * CONV_ROWS + (lead + w) - ph
            acc = acc + buf_scr[ph, base:base + CONV_ROWS, :] * jnp.tile(wdw_ref[w], (CONV_ROWS // SUBLANES, 1))
        mu = jnp.mean(acc, axis=-1, keepdims=True)
        cen = acc - mu
        var = jnp.mean(cen * cen, axis=-1, keepdims=True)
        y = cen * lax.rsqrt(var + NORM_EPS) * lg_ref[...] + lb_ref[...]
        y_scr[r * CONV_ROWS:(r + 1) * CONV_ROWS, :] = (y * jax.nn.sigmoid(y)).astype(BF16)
    o_ref[0] = h_ref[0] + (_dot(y_scr[...], wo_ref[...]) + bo_ref[...])


def _conv_main(u, h, w_dw, b_dw, ln_g, ln_b, w_out, b_out):
    B, S, D = h.shape
    ts = CONV_TS
    per = ts // CONV_HALO
    wdw = jnp.broadcast_to(w_dw[:, None, :], (CONV_WIDTH, SUBLANES, D))
    const = lambda a: pl.BlockSpec(a.shape, lambda b, i: (0,) * a.ndim)
    args = [wdw, b_dw[None, :], ln_g[None, :], ln_b[None, :], w_out.astype(BF16), b_out[None, :]]
    tile = pl.BlockSpec((1, ts, D), lambda b, i: (b, i, 0))
    return pl.pallas_call(
        _conv_main_kernel,
        grid=(B, S // ts),
        in_specs=[tile,
                  pl.BlockSpec((1, CONV_HALO, D), lambda b, i: (b, jnp.maximum(i * per - 1, 0), 0)),
                  tile] + [const(a) for a in args],
        out_specs=tile,
        out_shape=jax.ShapeDtypeStruct((B, S, D), F32),
        scratch_shapes=[pltpu.VMEM((SUBLANES, CONV_HALO + ts, D), F32), pltpu.VMEM((ts, D), BF16)],
        compiler_params=_params("parallel", "arbitrary"),
        name="conv_main",
    )(u, u, h, *args)


def _conv_layer(h, g, w_in, b_in, w_dw, b_dw, ln_g, ln_b, w_out, b_out):
    B, S, D = h.shape
    u = _conv_in(h.reshape(B * S, D), g, w_in, b_in).reshape(B, S, D)
    return _conv_main(u, h, w_dw, b_dw, ln_g, ln_b, w_out, b_out)


def _col_max(x):
    return jnp.max(x, axis=0, keepdims=True)


def _dup_bf16(x):
    bits = pltpu.bitcast(x.astype(BF16).astype(F32), jnp.uint32)
    return (bits & jnp.uint32(0xFFFF0000)) | (bits >> 16)


def _peer_route_kernel(h_ref, g_ref, wqh_ref, wql_ref, skh_ref, skl_ref,
                       xn_ref, cnt_ref, e1_ref, rank_ref, e2_ref, q_scr, top_scr):
    xn = _rms(h_ref[...], g_ref[...])
    xh, xl = _split(xn)
    xn_ref[...] = xh
    q = _dot3(xh, xl, wqh_ref[...], wql_ref[...])
    for hc in range(2 * PEER_HEADS):
        q_scr[hc] = q[:, hc * PEER_D_HALF:(hc + 1) * PEER_D_HALF]

    def head(h, carry):
        s = []
        rank = None
        for c in range(2):
            qh, ql = _split(q_scr[2 * h + c])
            sc = _dot3_nt(skh_ref[2 * h + c], skl_ref[2 * h + c], qh, ql)
            s.append(sc)
            rest = sc
            rank = jnp.full(sc.shape, NOT_RANKED, F32)
            for k in range(PEER_TOPK):
                top = _col_max(rest)
                top_scr[c, k:k + 1, :] = top
                hit = rest == top
                if c == 1:
                    rank = jnp.where(hit, float(k), rank)
                if k + 1 < PEER_TOPK:
                    rest = jnp.where(hit, -jnp.inf, rest)
        a = top_scr[0]
        b = top_scr[1]
        cands = [a[0:1] + b, a[1:2] + b[0:8]]
        cands += [a[i:i + 1] + b[0:8] for i in range(2, 8)]
        cands += [a[8:16] + b[0:1]]
        m_top = a[0:1] + b[0:1]
        z = jnp.zeros_like(m_top)
        thr = m_top
        for k in range(PEER_TOPK):
            part = jnp.maximum(cands[0][0:8], cands[0][8:16])
            for cnd in cands[1:]:
                part = jnp.maximum(part, cnd)
            thr = _col_max(part)
            z = z + jnp.exp(thr - m_top)
            if k + 1 < PEER_TOPK:
                cands = [jnp.where(cnd == thr, -jnp.inf, cnd) for cnd in cands]
        cnt = jnp.zeros(s[0].shape, F32)
        for j in range(PEER_TOPK):
            cnt = cnt + jnp.where(s[0] + b[j:j + 1] >= thr, 1.0, 0.0)
        cnt_ref[h] = _dup_bf16(cnt)
        e1_ref[h] = _dup_bf16(jnp.exp(s[0] - a[0:1]) * (0.5 / z))
        rank_ref[h] = rank.astype(BF16)
        e2_ref[h] = jnp.exp(s[1] - b[0:1]).astype(BF16)
        return carry

    lax.fori_loop(0, PEER_HEADS, head, 0)


def _peer_route(h2, g, wqh, wql, skh, skl):
    T, D = h2.shape
    tm = PEER_ROUTE_TM
    const = lambda a: pl.BlockSpec(a.shape, lambda i: (0,) * a.ndim)
    stat = pl.BlockSpec((PEER_HEADS, PEER_N_KEYS, tm), lambda i: (0, 0, i))
    a_shape = jax.ShapeDtypeStruct((PEER_HEADS, PEER_N_KEYS, T), jnp.uint32)
    b_shape = jax.ShapeDtypeStruct((PEER_HEADS, PEER_N_KEYS, T), BF16)
    args = [g[None, :], wqh, wql, skh, skl]
    return pl.pallas_call(
        _peer_route_kernel,
        grid=(T // tm,),
        in_specs=[pl.BlockSpec((tm, D), lambda i: (i, 0))] + [const(a) for a in args],
        out_specs=[pl.BlockSpec((tm, D), lambda i: (i, 0)), stat, stat, stat, stat],
        out_shape=[jax.ShapeDtypeStruct((T, D), BF16), a_shape, a_shape, b_shape, b_shape],
        scratch_shapes=[pltpu.VMEM((2 * PEER_HEADS, tm, PEER_D_HALF), F32),
                        pltpu.VMEM((2, PEER_TOPK, tm), F32)],
        compiler_params=_params("parallel"),
        name="peer_route",
    )(h2, *args)


def _gelu_x2(x):
    return x + x * lax.erf(x * (2.0 ** -0.5))


def _bcast_row(ref, h, a):
    row = ref[h, a:a + 1, :]
    return pltpu.bitcast(jnp.broadcast_to(row, (8, row.shape[1])), BF16)


def _peer_main_kernel(xn_ref, u_ref, vt_ref, cnt_ref, e1_ref, rank_ref, e2_ref, h_ref, o_ref, acc_scr, w_scr):
    k = pl.program_id(1)
    tm = w_scr.shape[1]
    pair = 2 * PEER_N_KEYS

    @pl.when(k == 0)
    def _():
        acc_scr[...] = jnp.zeros(acc_scr.shape, F32)

    for p in range(PEER_A_TILE // 2):
        prow = slice(p * pair, (p + 1) * pair)
        st = _dot_nt(u_ref[prow, :], xn_ref[...])
        for r in range(2):
            a = 2 * p + r
            gsum = None
            for h in range(PEER_HEADS):
                cnt = _bcast_row(cnt_ref, h, a)
                e1 = _bcast_row(e1_ref, h, a)
                reps = (PEER_N_KEYS // cnt.shape[0], 1)
                picked = rank_ref[h] < jnp.tile(cnt, reps)
                term = jnp.where(picked, e2_ref[h] * jnp.tile(e1, reps), jnp.zeros((), BF16))
                gsum = term if gsum is None else gsum + term
            act = _gelu_x2(st[r * PEER_N_KEYS:(r + 1) * PEER_N_KEYS, :].astype(BF16))
            w_scr[a * PEER_N_KEYS:(a + 1) * PEER_N_KEYS, :] = gsum * act
    acc_scr[...] += _dot(vt_ref[...], w_scr[...])

    @pl.when(k == pl.num_programs(1) - 1)
    def _():
        o_ref[...] = h_ref[...] + acc_scr[...].T


def _peer_main(h2, xn, u_bf, vt_bf, cnt, e1, rank, e2):
    T, D = h2.shape
    E = u_bf.shape[0]
    tm = PEER_TM
    te = PEER_A_TILE * PEER_N_KEYS
    tok = pl.BlockSpec((tm, D), lambda i, k: (i, 0))
    a_stat = pl.BlockSpec((PEER_HEADS, PEER_A_TILE, tm), lambda i, k: (0, k, i))
    b_stat = pl.BlockSpec((PEER_HEADS, PEER_N_KEYS, tm), lambda i, k: (0, 0, i))
    return pl.pallas_call(
        _peer_main_kernel,
        grid=(T // tm, E // te),
        in_specs=[tok,
                  pl.BlockSpec((te, D), lambda i, k: (k, 0)),
                  pl.BlockSpec((D, te), lambda i, k: (0, k)),
                  a_stat, a_stat, b_stat, b_stat, tok],
        out_specs=tok,
        out_shape=jax.ShapeDtypeStruct((T, D), F32),
        scratch_shapes=[pltpu.VMEM((D, tm), F32), pltpu.VMEM((te, tm), BF16)],
        compiler_params=_params("parallel", "arbitrary"),
        name="peer_main",
    )(xn, u_bf, vt_bf, cnt, e1, rank, e2, h2)


def _peer_layer(h, g, w_q, subkeys, u_tab, v_tab):
    B, S, D = h.shape
    h2 = h.reshape(B * S, D)
    wqh = w_q.astype(BF16)
    wql = (w_q - wqh.astype(F32)).astype(BF16)
    sk = subkeys.reshape(2 * PEER_HEADS, PEER_N_KEYS, PEER_D_HALF)
    skh = sk.astype(BF16)
    skl = (sk - skh.astype(F32)).astype(BF16)
    xn, cnt, e1, rank, e2 = _peer_route(h2, g, wqh, wql, skh, skl)
    out = _peer_main(h2, xn, u_tab.astype(BF16), v_tab.T.astype(BF16), cnt, e1, rank, e2)
    return out.reshape(B, S, D)


def _final_norm_kernel(h_ref, g_ref, o_ref):
    o_ref[...] = _rms(h_ref[...], g_ref[...])


def _final_norm(h2, g):
    T, D = h2.shape
    tm = 512
    return pl.pallas_call(
        _final_norm_kernel,
        grid=(T // tm,),
        in_specs=[pl.BlockSpec((tm, D), lambda i: (i, 0)), pl.BlockSpec((1, D), lambda i: (0, 0))],
        out_specs=pl.BlockSpec((tm, D), lambda i: (i, 0)),
        out_shape=jax.ShapeDtypeStruct((T, D), F32),
        compiler_params=_params("parallel"),
        name="final_norm",
    )(h2, g[None, :])


def kernel(x, positions, norm_mix, norm_ffn, norm_final, attn_w_in, mla_q_norm, mla_w_uq, mla_kv_norm,
           mla_w_ukv, attn_w_out, conv_w_in, conv_b_in, conv_w_dw, conv_b_dw, conv_ln_g, conv_ln_b,
           conv_w_out, conv_b_out, peer_w_q, peer_subkeys, peer_u, peer_v):
    B, S, D = x.shape
    assert D == D_MODEL and S % ATTN_TQ == 0 and S % MOBA_BLOCK == 0 and (B * S) % PEER_TM == 0
    depth = norm_mix.shape[0]
    tabs = _rope_tables(positions)
    h = x
    for layer in range(depth):
        i = layer // 2
        if layer % 2 == 0:
            w = _attn_weights(attn_w_in[i], mla_q_norm[i], mla_w_uq[i], mla_kv_norm[i], mla_w_ukv[i],
                              attn_w_out[i])
            h = _attention_layer(h, norm_mix[layer], w, tabs)
        else:
            h = _conv_layer(h, norm_mix[layer], conv_w_in[i], conv_b_in[i], conv_w_dw[i], conv_b_dw[i],
                            conv_ln_g[i], conv_ln_b[i], conv_w_out[i], conv_b_out[i])
        h = _peer_layer(h, norm_ffn[layer], peer_w_q[layer], peer_subkeys[layer], peer_u[layer],
                        peer_v[layer])
    return _final_norm(h.reshape(B * S, D), norm_final).reshape(B, S, D)
```

```python
import functools

import jax
import jax.numpy as jnp
import numpy as np
from jax import lax
from jax.experimental import pallas as pl
from jax.experimental.pallas import tpu as pltpu

F32 = jnp.float32
BF16 = jnp.bfloat16

D_MODEL = 1024
ROPE_THETA = 10000.0
NORM_EPS = 1e-6

MLA_HEADS = 8
MLA_Q_LORA = 384
MLA_KV_LORA = 256
MLA_NOPE = 64
MLA_ROPE = 32
MLA_V = 64

MOBA_HEADS = 8
MOBA_HEAD_DIM = 64
MOBA_BLOCK = 256
MOBA_TOPK = 3

CONV_WIDTH = 31

PEER_HEADS = 8
PEER_N_KEYS = 128
PEER_D_HALF = 128
PEER_TOPK = 16

LANES = 128
SUBLANES = 8
HEAD_PAIRS = 4
VMEM_LIMIT = 56 * 1024 * 1024
NEG = -1e30
LOG2E = 1.4426950408889634

ATTN_TQ = 512
ATTN_TK = 512
FLASH_PAIRS = 2
PROJ_TM = 256
PEER_ROUTE_TM = 256
PEER_TM = 512
PEER_A_TILE = 32
NOT_RANKED = 255.0
CONV_TS = 512
CONV_HALO = 32
CONV_ROWS = 32


def _params(*sem):
    return pltpu.CompilerParams(dimension_semantics=sem, vmem_limit_bytes=VMEM_LIMIT)


def _dot(a, b):
    return jnp.dot(a, b, preferred_element_type=F32)


def _dot_nt(a, b):
    return lax.dot_general(a, b, (((1,), (1,)), ((), ())), preferred_element_type=F32)


def _split(a):
    hi = a.astype(BF16)
    lo = (a - hi.astype(F32)).astype(BF16)
    return hi, lo


def _dot3(ah, al, bh, bl):
    return _dot(ah, bh) + (_dot(ah, bl) + _dot(al, bh))


def _dot3_nt(ah, al, bh, bl):
    return _dot_nt(ah, bh) + (_dot_nt(ah, bl) + _dot_nt(al, bh))


def _rms(x, g):
    return x * lax.rsqrt(jnp.mean(x * x, axis=-1, keepdims=True) + NORM_EPS) * g


def _rope_kernel(pos_ref, fr_ref, fm_ref, cr_ref, sr_ref, cm_ref, sm_ref):
    pos = pos_ref[0].astype(F32)
    ang_r = pos * fr_ref[0:1, :]
    ang_m = pos * fm_ref[0:1, :]
    cr_ref[0] = jnp.cos(ang_r) * fr_ref[1:2, :] + fr_ref[2:3, :]
    sr_ref[0] = jnp.sin(ang_r) * fr_ref[1:2, :]
    cm_ref[0] = jnp.cos(ang_m)
    sm_ref[0] = jnp.sin(ang_m) * fm_ref[1:2, :]


def _rope_tables(positions):
    B, S = positions.shape
    ts = 512
    inv_r = 1.0 / (ROPE_THETA ** (jnp.arange(0, MLA_ROPE, 2, dtype=F32) / MLA_ROPE))
    inv_m = 1.0 / (ROPE_THETA ** (jnp.arange(0, MOBA_HEAD_DIM, 2, dtype=F32) / MOBA_HEAD_DIM))
    z32 = jnp.zeros((32,), F32)
    z64 = jnp.zeros((64,), F32)
    fr = jnp.stack([jnp.concatenate([z64, inv_r, inv_r, z32]),
                    jnp.concatenate([z64, jnp.ones((32,), F32), z32]),
                    jnp.concatenate([jnp.ones((64,), F32), z32, z32])])
    fr = jnp.concatenate([fr, jnp.zeros((5, LANES), F32)])
    sign = jnp.tile(jnp.concatenate([-jnp.ones((32,), F32), jnp.ones((32,), F32)]), 2)
    fm = jnp.concatenate([jnp.tile(inv_m, 4)[None, :], sign[None, :], jnp.zeros((6, LANES), F32)])
    tab = jax.ShapeDtypeStruct((B, S, LANES), F32)
    row = pl.BlockSpec((1, ts, LANES), lambda b, i: (b, i, 0))
    return pl.pallas_call(
        _rope_kernel,
        grid=(B, S // ts),
        in_specs=[pl.BlockSpec((1, ts, 1), lambda b, i: (b, i, 0)),
                  pl.BlockSpec((8, LANES), lambda b, i: (0, 0)),
                  pl.BlockSpec((8, LANES), lambda b, i: (0, 0))],
        out_specs=[row, row, row, row],
        out_shape=[tab, tab, tab, tab],
        compiler_params=_params("parallel", "parallel"),
        name="rope_tables",
    )(positions.reshape(B, S, 1), fr, fm)


def _attn_prep_kernel(h_ref, g_ref, w1_ref, w2h_ref, w2l_ref, gq_ref, gkv_ref, wq_ref, wqr_ref,
                      wk_ref, wv_ref, cr_ref, sr_ref, cm_ref, sm_ref,
                      qm_ref, km_ref, vm_ref, qb_ref, qf_ref, kf_ref, kb_ref, vb_ref):
    xn = _rms(h_ref[0], g_ref[...])
    xh, xl = _split(xn)
    p1 = _dot(xh, w1_ref[...])
    p2 = _dot3(xh, xl, w2h_ref[...], w2l_ref[...])
    cq, ckv = p1[:, 0:384], p1[:, 384:640]
    kr1, kr2 = p1[:, 640:768], p1[:, 768:896]
    mv = p1[:, 896:1408]
    cr, sr, cm, sm = cr_ref[0], sr_ref[0], cm_ref[0], sm_ref[0]
    nq = _rms(cq, gq_ref[...]).astype(BF16)
    nkv = _rms(ckv, gkv_ref[...]).astype(BF16)
    qa = _dot(nq, wq_ref[...])
    qr = _dot(nq, wqr_ref[...])
    ka = _dot(nkv, wk_ref[...])
    va = _dot(nkv, wv_ref[...])
    kpe = kr1 * cr + kr2 * sr
    mla_scale = (MLA_NOPE + MLA_ROPE) ** -0.5 * LOG2E
    for h in range(MLA_HEADS):
        sl = slice(h * LANES, (h + 1) * LANES)
        qm_ref[0, h] = ((qa[:, sl] * cr + qr[:, sl] * sr) * mla_scale).astype(BF16)
        km_ref[0, h] = (ka[:, sl] + kpe).astype(BF16)
        vm_ref[0, h] = va[:, sl].astype(BF16)
    lane = lax.broadcasted_iota(jnp.int32, cm.shape, 1)
    low = lane < MOBA_HEAD_DIM
    first = (lane & (MOBA_HEAD_DIM - 1)) < MOBA_HEAD_DIM // 2

    def rope(x):
        half = MOBA_HEAD_DIM // 2
        return x * cm + jnp.where(first, pltpu.roll(x, LANES - half, axis=1), pltpu.roll(x, half, axis=1)) * sm

    moba_scale = MOBA_HEAD_DIM ** -0.5 * LOG2E
    for p in range(HEAD_PAIRS):
        sl = slice(p * LANES, (p + 1) * LANES)
        q = rope(p2[:, sl])
        k = rope(p2[:, 512 + p * LANES:512 + (p + 1) * LANES])
        v = mv[:, sl]
        qf_ref[0, :, sl] = q
        kf_ref[0, :, sl] = k
        qb_ref[0, :, sl] = (q * moba_scale).astype(BF16)
        kb_ref[0, 2 * p] = jnp.where(low, k, 0.0).astype(BF16)
        kb_ref[0, 2 * p + 1] = jnp.where(low, 0.0, k).astype(BF16)
        vb_ref[0, 2 * p] = jnp.where(low, v, 0.0).astype(BF16)
        vb_ref[0, 2 * p + 1] = jnp.where(low, 0.0, v).astype(BF16)


def _attn_prep(h, g, w, tabs):
    B, S, D = h.shape
    tm = PROJ_TM
    const = lambda a: pl.BlockSpec(a.shape, lambda b, i: (0,) * a.ndim)
    row = lambda n: pl.BlockSpec((1, tm, n), lambda b, i: (b, i, 0))
    head = pl.BlockSpec((1, 8, tm, LANES), lambda b, i: (b, 0, i, 0))
    head_shape = jax.ShapeDtypeStruct((B, 8, S, LANES), BF16)
    weights = [g, w["w1"], w["w2h"], w["w2l"], w["gq"], w["gkv"], w["wq"], w["wqr"], w["wk"], w["wv"]]
    return pl.pallas_call(
        _attn_prep_kernel,
        grid=(B, S // tm),
        in_specs=[row(D)] + [const(a) for a in weights] + [row(LANES)] * 4,
        out_specs=[head, head, head, row(512), row(512), row(512), head, head],
        out_shape=[head_shape, head_shape, head_shape,
                   jax.ShapeDtypeStruct((B, S, 512), BF16),
                   jax.ShapeDtypeStruct((B, S, 512), F32),
                   jax.ShapeDtypeStruct((B, S, 512), F32),
                   head_shape, head_shape],
        compiler_params=_params("parallel", "parallel"),
        name="attn_prep",
    )(h, *weights, *tabs)


def _moba_gate_kernel(q_ref, k_ref, sel_ref, kmh_ref, kml_ref):
    i = pl.program_id(2)
    tq = q_ref.shape[1]
    S = k_ref.shape[1]
    nb = S // MOBA_BLOCK

    @pl.when(i == 0)
    def _():
        km = jnp.sum(k_ref[0].reshape(nb, MOBA_BLOCK, LANES), axis=1) / float(MOBA_BLOCK)
        low = lax.broadcasted_iota(jnp.int32, km.shape, 1) < MOBA_HEAD_DIM
        for hh, m in enumerate((jnp.where(low, km, 0.0), jnp.where(low, 0.0, km))):
            hi, lo = _split(m)
            kmh_ref[hh] = hi
            kml_ref[hh] = lo

    qh, ql = _split(q_ref[0])
    row = i * tq + lax.broadcasted_iota(jnp.int32, (tq, nb), 0)
    blk = lax.broadcasted_iota(jnp.int32, (tq, nb), 1)
    valid = blk * MOBA_BLOCK < row - (row & (MOBA_BLOCK - 1))
    for hh in range(2):
        gate = _dot3_nt(qh, ql, kmh_ref[hh], kml_ref[hh])
        g = jnp.where(valid, gate, -jnp.inf)
        rest = g
        for _ in range(MOBA_TOPK - 1):
            top = jnp.max(rest, axis=1, keepdims=True)
            rest = jnp.where(rest == top, -jnp.inf, rest)
        thr = jnp.max(rest, axis=1, keepdims=True)
        sel_ref[0, hh] = jnp.where(valid & (g >= thr), 1.0, 0.0)


def _moba_gate(qf, kf):
    B, S, _ = qf.shape
    tq = ATTN_TQ
    nb = S // MOBA_BLOCK
    return pl.pallas_call(
        _moba_gate_kernel,
        grid=(B, HEAD_PAIRS, S // tq),
        in_specs=[pl.BlockSpec((1, tq, LANES), lambda b, p, i: (b, i, p)),
                  pl.BlockSpec((1, S, LANES), lambda b, p, i: (b, 0, p))],
        out_specs=pl.BlockSpec((1, 2, tq, nb), lambda b, p, i: (b, p, i, 0)),
        out_shape=jax.ShapeDtypeStruct((B, 8, S, nb), F32),
        scratch_shapes=[pltpu.VMEM((2, nb, LANES), BF16), pltpu.VMEM((2, nb, LANES), BF16)],
        compiler_params=_params("parallel", "parallel", "arbitrary"),
        name="moba_gate",
    )(qf, kf)


def _flash_kernel(*refs, moba, paired_q):
    if moba:
        q_ref, k_ref, v_ref, sel_ref, o_ref, m_scr, l_scr, acc_scr = refs
    else:
        q_ref, k_ref, v_ref, o_ref, m_scr, l_scr, acc_scr = refs
    i = pl.program_id(2)
    tq, tk = ATTN_TQ, ATTN_TK
    heads = 2 * FLASH_PAIRS
    if paired_q:
        qs = [q_ref[0, :, (hd // 2) * LANES:(hd // 2 + 1) * LANES] for hd in range(heads)]
    else:
        qs = [q_ref[0, hd] for hd in range(heads)]
    m_scr[...] = jnp.full(m_scr.shape, NEG, F32)
    l_scr[...] = jnp.zeros(l_scr.shape, F32)
    acc_scr[...] = jnp.zeros(acc_scr.shape, F32)
    low = lax.broadcasted_iota(jnp.int32, (tq, LANES), 1) < 64

    def step(j, diag):
        start = pl.multiple_of(j * tk, tk)
        for pp in range(FLASH_PAIRS):
            alphas, pvs = [], []
            for hh in range(2):
                hd = 2 * pp + hh
                k = k_ref[0, hd, pl.ds(start, tk), :]
                v = v_ref[0, hd, pl.ds(start, tk), :]
                s = _dot_nt(qs[hd], k)
                if moba:
                    sel = sel_ref[0, hd]
                    col = lax.broadcasted_iota(jnp.int32, sel.shape, 1)
                    sub = tk // MOBA_BLOCK
                    picked = jnp.concatenate(
                        [jnp.broadcast_to(
                            jnp.sum(jnp.where(col == j * sub + r, sel, 0.0), axis=1, keepdims=True),
                            (tq, MOBA_BLOCK)) for r in range(sub)], axis=1) > 0.5
                if diag:
                    qpos = i * tq + lax.broadcasted_iota(jnp.int32, (tq, tk), 0)
                    kpos = j * tk + lax.broadcasted_iota(jnp.int32, (tq, tk), 1)
                    ok = kpos <= qpos
                    if moba:
                        ok = (ok & (kpos >= qpos - (qpos & (MOBA_BLOCK - 1)))) | picked
                    s = jnp.where(ok, s, NEG)
                elif moba:
                    s = jnp.where(picked, s, NEG)
                m_prev = m_scr[hd]
                m_next = jnp.maximum(m_prev, jnp.max(s, axis=1, keepdims=True))
                alpha = jnp.exp2(m_prev - m_next)
                p = jnp.exp2(s - jnp.tile(m_next, (1, tk // LANES)))
                l_scr[hd] = alpha * l_scr[hd] + jnp.sum(p, axis=1, keepdims=True)
                m_scr[hd] = m_next
                alphas.append(alpha)
                pvs.append(_dot(p.astype(BF16), v))
            acc_scr[pp] = acc_scr[pp] * jnp.where(low, alphas[0], alphas[1]) + (pvs[0] + pvs[1])

    n_full = i * (tq // tk)

    def body(j, carry):
        step(j, False)
        return carry

    lax.fori_loop(0, n_full, body, 0)
    for r in range(tq // tk):
        step(n_full + r, True)
    for pp in range(FLASH_PAIRS):
        o_ref[0, :, pp * LANES:(pp + 1) * LANES] = (
            acc_scr[pp] / jnp.where(low, l_scr[2 * pp], l_scr[2 * pp + 1])).astype(o_ref.dtype)


def _flash(q, k, v, sel=None):
    moba = sel is not None
    B, _, S, _ = k.shape
    tq = ATTN_TQ
    np_, heads = FLASH_PAIRS, 2 * FLASH_PAIRS
    kv_spec = pl.BlockSpec((1, heads, S, LANES), lambda b, p, i: (b, p, 0, 0))
    if moba:
        q_spec = pl.BlockSpec((1, tq, np_ * LANES), lambda b, p, i: (b, i, p))
        extra = [pl.BlockSpec((1, heads, tq, sel.shape[3]), lambda b, p, i: (b, p, i, 0))]
        args = (q, k, v, sel)
    else:
        q_spec = pl.BlockSpec((1, heads, tq, LANES), lambda b, p, i: (b, p, i, 0))
        extra = []
        args = (q, k, v)
    return pl.pallas_call(
        functools.partial(_flash_kernel, moba=moba, paired_q=moba),
        grid=(B, HEAD_PAIRS // np_, S // tq),
        in_specs=[q_spec, kv_spec, kv_spec] + extra,
        out_specs=pl.BlockSpec((1, tq, np_ * LANES), lambda b, p, i: (b, i, p)),
        out_shape=jax.ShapeDtypeStruct((B, S, HEAD_PAIRS * LANES), BF16),
        scratch_shapes=[pltpu.VMEM((heads, tq, LANES), F32), pltpu.VMEM((heads, tq, LANES), F32),
                        pltpu.VMEM((np_, tq, LANES), F32)],
        compiler_params=_params("parallel", "parallel", "arbitrary"),
        name="flash_moba" if moba else "flash_mla",
    )(*args)


def _attn_out_kernel(h_ref, oa_ref, ob_ref, w_ref, o_ref):
    half = oa_ref.shape[1]
    o_ref[...] = h_ref[...] + (_dot(oa_ref[...], w_ref[0:half, :]) + _dot(ob_ref[...], w_ref[half:, :]))


def _attn_out(h2, o_mla, o_moba, w_out):
    T, D = h2.shape
    tm = 512
    half = o_mla.shape[1]
    return pl.pallas_call(
        _attn_out_kernel,
        grid=(T // tm,),
        in_specs=[pl.BlockSpec((tm, D), lambda i: (i, 0)),
                  pl.BlockSpec((tm, half), lambda i: (i, 0)),
                  pl.BlockSpec((tm, half), lambda i: (i, 0)),
                  pl.BlockSpec(w_out.shape, lambda i: (0, 0))],
        out_specs=pl.BlockSpec((tm, D), lambda i: (i, 0)),
        out_shape=jax.ShapeDtypeStruct((T, D), F32),
        compiler_params=_params("parallel"),
        name="attn_out",
    )(h2, o_mla, o_moba, w_out)


def _attn_weights(w_in, q_norm, w_uq, kv_norm, w_ukv, w_out):
    D = w_in.shape[0]
    w_cq, w_ckv = w_in[:, 0:384], w_in[:, 384:640]
    w_kr = w_in[:, 640:672]
    w_mq, w_mk, w_mv = w_in[:, 672:1184], w_in[:, 1184:1696], w_in[:, 1696:2208]
    z = lambda n: jnp.zeros((D, n), F32)
    kr1 = jnp.concatenate([z(64), w_kr, z(32)], axis=1)
    kr2 = jnp.concatenate([z(64), -w_kr[:, 16:32], w_kr[:, 0:16], z(32)], axis=1)
    w1 = jnp.concatenate([w_cq, w_ckv, kr1, kr2, w_mv], axis=1).astype(BF16)

    w2 = jnp.concatenate([w_mq, w_mk], axis=1)
    w2h = w2.astype(BF16)
    w2l = (w2 - w2h.astype(F32)).astype(BF16)

    uq = w_uq.reshape(MLA_Q_LORA, MLA_HEADS, MLA_NOPE + MLA_ROPE)
    nope, pe1, pe2 = uq[..., :64], uq[..., 64:80], uq[..., 80:96]
    zq = lambda n: jnp.zeros((MLA_Q_LORA, MLA_HEADS, n), F32)
    wq = jnp.concatenate([nope, pe1, pe2, zq(32)], axis=-1).reshape(MLA_Q_LORA, 1024).astype(BF16)
    wqr = jnp.concatenate([zq(64), -pe2, pe1, zq(32)], axis=-1).reshape(MLA_Q_LORA, 1024).astype(BF16)

    ukv = w_ukv.reshape(MLA_KV_LORA, MLA_HEADS, MLA_NOPE + MLA_V)
    k_nope, v = ukv[..., :64], ukv[..., 64:]
    zk = jnp.zeros((MLA_KV_LORA, MLA_HEADS, 64), F32)
    wk = jnp.concatenate([k_nope, zk], axis=-1).reshape(MLA_KV_LORA, 1024).astype(BF16)
    even = (jnp.arange(MLA_HEADS) % 2 == 0)[None, :, None]
    wv = jnp.concatenate([jnp.where(even, v, 0.0), jnp.where(even, 0.0, v)], axis=-1)
    wv = wv.reshape(MLA_KV_LORA, 1024).astype(BF16)
    return dict(w1=w1, w2h=w2h, w2l=w2l, gq=q_norm[None, :], gkv=kv_norm[None, :], wq=wq, wqr=wqr,
                wk=wk, wv=wv, w_out=w_out.astype(BF16))


def _attention_layer(h, g, w, tabs):
    B, S, D = h.shape
    qm, km, vm, qb, qf, kf, kb, vb = _attn_prep(h, g[None, :], w, tabs)
    o_mla = _flash(qm, km, vm)
    sel = _moba_gate(qf, kf)
    o_moba = _flash(qb, kb, vb, sel)
    out = _attn_out(h.reshape(B * S, D), o_mla.reshape(B * S, -1), o_moba.reshape(B * S, -1), w["w_out"])
    return out.reshape(B, S, D)


def _conv_in_kernel(h_ref, g_ref, wu_ref, wg_ref, bu_ref, bg_ref, o_ref):
    xn = _rms(h_ref[...], g_ref[...]).astype(BF16)
    u = _dot(xn, wu_ref[...]) + bu_ref[...]
    gate = _dot(xn, wg_ref[...]) + bg_ref[...]
    o_ref[...] = u * jax.nn.sigmoid(gate)


def _conv_in(h2, g, w_in, b_in):
    T, D = h2.shape
    tm = 512
    const = lambda a: pl.BlockSpec(a.shape, lambda i: (0, 0))
    args = [g[None, :], w_in[:, :D].astype(BF16), w_in[:, D:].astype(BF16), b_in[None, :D], b_in[None, D:]]
    return pl.pallas_call(
        _conv_in_kernel,
        grid=(T // tm,),
        in_specs=[pl.BlockSpec((tm, D), lambda i: (i, 0))] + [const(a) for a in args],
        out_specs=pl.BlockSpec((tm, D), lambda i: (i, 0)),
        out_shape=jax.ShapeDtypeStruct((T, D), F32),
        compiler_params=_params("parallel"),
        name="conv_in",
    )(h2, *args)


def _conv_main_kernel(u_ref, up_ref, h_ref, wdw_ref, bdw_ref, lg_ref, lb_ref, wo_ref, bo_ref, o_ref,
                      buf_scr, y_scr):
    i = pl.program_id(1)
    ts = u_ref.shape[1]
    rows = CONV_HALO + ts
    buf_scr[0, 0:CONV_HALO, :] = jnp.where(i == 0, 0.0, up_ref[0])
    buf_scr[0, CONV_HALO:rows, :] = u_ref[0]
    for ph in range(1, SUBLANES):
        buf_scr[ph] = pltpu.roll(buf_scr[0], rows - ph, axis=0)
    lead = CONV_HALO - (CONV_WIDTH - 1)
    for r in range(ts // CONV_ROWS):
        acc = jnp.broadcast_to(bdw_ref[...], (CONV_ROWS, bdw_ref.shape[1]))
        for w in range(CONV_WIDTH):
            ph = (lead + w) % SUBLANES
            base = r * CONV_ROWS + (lead + w) - ph
            acc = acc + buf_scr[ph, base:base + CONV_ROWS, :] * jnp.tile(wdw_ref[w], (CONV_ROWS // SUBLANES, 1))
        mu = jnp.mean(acc, axis=-1, keepdims=True)
        cen = acc - mu
        var = jnp.mean(cen * cen, axis=-1, keepdims=True)
        y = cen * lax.rsqrt(var + NORM_EPS) * lg_ref[...] + lb_ref[...]
        y_scr[r * CONV_ROWS:(r + 1) * CONV_ROWS, :] = (y * jax.nn.sigmoid(y)).astype(BF16)
    o_ref[0] = h_ref[0] + (_dot(y_scr[...], wo_ref[...]) + bo_ref[...])


def _conv_main(u, h, w_dw, b_dw, ln_g, ln_b, w_out, b_out):
    B, S, D = h.shape
    ts = CONV_TS
    per = ts // CONV_HALO
    wdw = jnp.broadcast_to(w_dw[:, None, :], (CONV_WIDTH, SUBLANES, D))
    const = lambda a: pl.BlockSpec(a.shape, lambda b, i: (0,) * a.ndim)
    args = [wdw, b_dw[None, :], ln_g[None, :], ln_b[None, :], w_out.astype(BF16), b_out[None, :]]
    tile = pl.BlockSpec((1, ts, D), lambda b, i: (b, i, 0))
    return pl.pallas_call(
        _conv_main_kernel,
        grid=(B, S // ts),
        in_specs=[tile,
                  pl.BlockSpec((1, CONV_HALO, D), lambda b, i: (b, jnp.maximum(i * per - 1, 0), 0)),
                  tile] + [const(a) for a in args],
        out_specs=tile,
        out_shape=jax.ShapeDtypeStruct((B, S, D), F32),
        scratch_shapes=[pltpu.VMEM((SUBLANES, CONV_HALO + ts, D), F32), pltpu.VMEM((ts, D), BF16)],
        compiler_params=_params("parallel", "arbitrary"),
        name="conv_main",
    )(u, u, h, *args)


def _conv_layer(h, g, w_in, b_in, w_dw, b_dw, ln_g, ln_b, w_out, b_out):
    B, S, D = h.shape
    u = _conv_in(h.reshape(B * S, D), g, w_in, b_in).reshape(B, S, D)
    return _conv_main(u, h, w_dw, b_dw, ln_g, ln_b, w_out, b_out)


def _col_max(x):
    return jnp.max(x, axis=0, keepdims=True)


def _dup_bf16(x):
    bits = pltpu.bitcast(x.astype(BF16).astype(F32), jnp.uint32)
    return (bits & jnp.uint32(0xFFFF0000)) | (bits >> 16)


def _peer_route_kernel(h_ref, g_ref, wqh_ref, wql_ref, skh_ref, skl_ref,
                       xn_ref, cnt_ref, e1_ref, rank_ref, e2_ref, q_scr, top_scr):
    xn = _rms(h_ref[...], g_ref[...])
    xh, xl = _split(xn)
    xn_ref[...] = xh
    q = _dot3(xh, xl, wqh_ref[...], wql_ref[...])
    for hc in range(2 * PEER_HEADS):
        q_scr[hc] = q[:, hc * PEER_D_HALF:(hc + 1) * PEER_D_HALF]

    def head(h, carry):
        s = []
        rank = None
        for c in range(2):
            qh, ql = _split(q_scr[2 * h + c])
            sc = _dot3_nt(skh_ref[2 * h + c], skl_ref[2 * h + c], qh, ql)
            s.append(sc)
            rest = sc
            rank = jnp.full(sc.shape, NOT_RANKED, F32)
            for k in range(PEER_TOPK):
                top = _col_max(rest)
                top_scr[c, k:k + 1, :] = top
                hit = rest == top
                if c == 1:
                    rank = jnp.where(hit, float(k), rank)
                if k + 1 < PEER_TOPK:
                    rest = jnp.where(hit, -jnp.inf, rest)
        a = top_scr[0]
        b = top_scr[1]
        cands = [a[0:1] + b, a[1:2] + b[0:8]]
        cands += [a[i:i + 1] + b[0:8] for i in range(2, 8)]
        cands += [a[8:16] + b[0:1]]
        m_top = a[0:1] + b[0:1]
        z = jnp.zeros_like(m_top)
        thr = m_top
        for k in range(PEER_TOPK):
            part = jnp.maximum(cands[0][0:8], cands[0][8:16])
            for cnd in cands[1:]:
                part = jnp.maximum(part, cnd)
            thr = _col_max(part)
            z = z + jnp.exp(thr - m_top)
            if k + 1 < PEER_TOPK:
                cands = [jnp.where(cnd == thr, -jnp.inf, cnd) for cnd in cands]
        cnt = jnp.zeros(s[0].shape, F32)
        for j in range(PEER_TOPK):
            cnt = cnt + jnp.where(s[0] + b[j:j + 1] >= thr, 1.0, 0.0)
        cnt_ref[h] = _dup_bf16(cnt)
        e1_ref[h] = _dup_bf16(jnp.exp(s[0] - a[0:1]) * (0.5 / z))
        rank_ref[h] = rank.astype(BF16)
        e2_ref[h] = jnp.exp(s[1] - b[0:1]).astype(BF16)
        return carry

    lax.fori_loop(0, PEER_HEADS, head, 0)


def _peer_route(h2, g, wqh, wql, skh, skl):
    T, D = h2.shape
    tm = PEER_ROUTE_TM
    const = lambda a: pl.BlockSpec(a.shape, lambda i: (0,) * a.ndim)
    stat = pl.BlockSpec((PEER_HEADS, PEER_N_KEYS, tm), lambda i: (0, 0, i))
    a_shape = jax.ShapeDtypeStruct((PEER_HEADS, PEER_N_KEYS, T), jnp.uint32)
    b_shape = jax.ShapeDtypeStruct((PEER_HEADS, PEER_N_KEYS, T), BF16)
    args = [g[None, :], wqh, wql, skh, skl]
    return pl.pallas_call(
        _peer_route_kernel,
        grid=(T // tm,),
        in_specs=[pl.BlockSpec((tm, D), lambda i: (i, 0))] + [const(a) for a in args],
        out_specs=[pl.BlockSpec((tm, D), lambda i: (i, 0)), stat, stat, stat, stat],
        out_shape=[jax.ShapeDtypeStruct((T, D), BF16), a_shape, a_shape, b_shape, b_shape],
        scratch_shapes=[pltpu.VMEM((2 * PEER_HEADS, tm, PEER_D_HALF), F32),
                        pltpu.VMEM((2, PEER_TOPK, tm), F32)],
        compiler_params=_params("parallel"),
        name="peer_route",
    )(h2, *args)


def _gelu_x2(x):
    return x + x * lax.erf(x * (2.0 ** -0.5))


def _bcast_row(ref, h, a):
    row = ref[h, a:a + 1, :]
    return pltpu.bitcast(jnp.broadcast_to(row, (8, row.shape[1])), BF16)


def _peer_main_kernel(xn_ref, u_ref, vt_ref, cnt_ref, e1_ref, rank_ref, e2_ref, h_ref, o_ref, acc_scr, w_scr):
    k = pl.program_id(1)
    tm = w_scr.shape[1]
    pair = 2 * PEER_N_KEYS

    @pl.when(k == 0)
    def _():
        acc_scr[...] = jnp.zeros(acc_scr.shape, F32)

    for p in range(PEER_A_TILE // 2):
        prow = slice(p * pair, (p + 1) * pair)
        st = _dot_nt(u_ref[prow, :], xn_ref[...])
        for r in range(2):
            a = 2 * p + r
            gsum = None
            for h in range(PEER_HEADS):
                cnt = _bcast_row(cnt_ref, h, a)
                e1 = _bcast_row(e1_ref, h, a)
                reps = (PEER_N_KEYS // cnt.shape[0], 1)
                picked = rank_ref[h] < jnp.tile(cnt, reps)
                term = jnp.where(picked, e2_ref[h] * jnp.tile(e1, reps), jnp.zeros((), BF16))
                gsum = term if gsum is None else gsum + term
            act = _gelu_x2(st[r * PEER_N_KEYS:(r + 1) * PEER_N_KEYS, :].astype(BF16))
            w_scr[a * PEER_N_KEYS:(a + 1) * PEER_N_KEYS, :] = gsum * act
    acc_scr[...] += _dot(vt_ref[...], w_scr[...])

    @pl.when(k == pl.num_programs(1) - 1)
    def _():
        o_ref[...] = h_ref[...] + acc_scr[...].T


def _peer_main(h2, xn, u_bf, vt_bf, cnt, e1, rank, e2):
    T, D = h2.shape
    E = u_bf.shape[0]
    tm = PEER_TM
    te = PEER_A_TILE * PEER_N_KEYS
    tok = pl.BlockSpec((tm, D), lambda i, k: (i, 0))
    a_stat = pl.BlockSpec((PEER_HEADS, PEER_A_TILE, tm), lambda i, k: (0, k, i))
    b_stat = pl.BlockSpec((PEER_HEADS, PEER_N_KEYS, tm), lambda i, k: (0, 0, i))
    return pl.pallas_call(
        _peer_main_kernel,
        grid=(T // tm, E // te),
        in_specs=[tok,
                  pl.BlockSpec((te, D), lambda i, k: (k, 0)),
                  pl.BlockSpec((D, te), lambda i, k: (0, k)),
                  a_stat, a_stat, b_stat, b_stat, tok],
        out_specs=tok,
        out_shape=jax.ShapeDtypeStruct((T, D), F32),
        scratch_shapes=[pltpu.VMEM((D, tm), F32), pltpu.VMEM((te, tm), BF16)],
        compiler_params=_params("parallel", "arbitrary"),
        name="peer_main",
    )(xn, u_bf, vt_bf, cnt, e1, rank, e2, h2)


def _peer_layer(h, g, w_q, subkeys, u_tab, v_tab):
    B, S, D = h.shape
    h2 = h.reshape(B * S, D)
    wqh = w_q.astype(BF16)
    wql = (w_q - wqh.astype(F32)).astype(BF16)
    sk = subkeys.reshape(2 * PEER_HEADS, PEER_N_KEYS, PEER_D_HALF)
    skh = sk.astype(BF16)
    skl = (sk - skh.astype(F32)).astype(BF16)
    xn, cnt, e1, rank, e2 = _peer_route(h2, g, wqh, wql, skh, skl)
    out = _peer_main(h2, xn, u_tab.astype(BF16), v_tab.T.astype(BF16), cnt, e1, rank, e2)
    return out.reshape(B, S, D)


def _final_norm_kernel(h_ref, g_ref, o_ref):
    o_ref[...] = _rms(h_ref[...], g_ref[...])


def _final_norm(h2, g):
    T, D = h2.shape
    tm = 512
    return pl.pallas_call(
        _final_norm_kernel,
        grid=(T // tm,),
        in_specs=[pl.BlockSpec((tm, D), lambda i: (i, 0)), pl.BlockSpec((1, D), lambda i: (0, 0))],
        out_specs=pl.BlockSpec((tm, D), lambda i: (i, 0)),
        out_shape=jax.ShapeDtypeStruct((T, D), F32),
        compiler_params=_params("parallel"),
        name="final_norm",
    )(h2, g[None, :])


def kernel(x, positions, norm_mix, norm_ffn, norm_final, attn_w_in, mla_q_norm, mla_w_uq, mla_kv_norm,
           mla_w_ukv, attn_w_out, conv_w_in, conv_b_in, conv_w_dw, conv_b_dw, conv_ln_g, conv_ln_b,
           conv_w_out, conv_b_out, peer_w_q, peer_subkeys, peer_u, peer_v):
    B, S, D = x.shape
    assert D == D_MODEL and S % ATTN_TQ == 0 and S % MOBA_BLOCK == 0 and (B * S) % PEER_TM == 0
    depth = norm_mix.shape[0]
    tabs = _rope_tables(positions)
    h = x
    for layer in range(depth):
        i = layer // 2
        if layer % 2 == 0:
            w = _attn_weights(attn_w_in[i], mla_q_norm[i], mla_w_uq[i], mla_kv_norm[i], mla_w_ukv[i],
                              attn_w_out[i])
            h = _attention_layer(h, norm_mix[layer], w, tabs)
        else:
            h = _conv_layer(h, norm_mix[layer], conv_w_in[i], conv_b_in[i], conv_w_dw[i], conv_b_dw[i],
                            conv_ln_g[i], conv_ln_b[i], conv_w_out[i], conv_b_out[i])
        h = _peer_layer(h, norm_ffn[layer], peer_w_q[layer], peer_subkeys[layer], peer_u[layer],
                        peer_v[layer])
    return _final_norm(h.reshape(B * S, D), norm_final).reshape(B, S, D)
```

```python
import functools

import jax
import jax.numpy as jnp
import numpy as np
from jax import lax
from jax.experimental import pallas as pl
from jax.experimental.pallas import tpu as pltpu

F32 = jnp.float32
BF16 = jnp.bfloat16

D_MODEL = 1024
ROPE_THETA = 10000.0
NORM_EPS = 1e-6

MLA_HEADS = 8
MLA_Q_LORA = 384
MLA_KV_LORA = 256
MLA_NOPE = 64
MLA_ROPE = 32
MLA_V = 64

MOBA_HEADS = 8
MOBA_HEAD_DIM = 64
MOBA_BLOCK = 256
MOBA_TOPK = 3

CONV_WIDTH = 31

PEER_HEADS = 8
PEER_N_KEYS = 128
PEER_D_HALF = 128
PEER_TOPK = 16

LANES = 128
SUBLANES = 8
HEAD_PAIRS = 4
VMEM_LIMIT = 56 * 1024 * 1024
NEG = -1e30
LOG2E = 1.4426950408889634

ATTN_TQ = 512
ATTN_TK = 512
FLASH_PAIRS = 2
PROJ_TM = 256
PEER_ROUTE_TM = 256
PEER_TM = 512
PEER_A_TILE = 32
CONV_TS = 512
CONV_HALO = 32
CONV_ROWS = 32


def _params(*sem):
    return pltpu.CompilerParams(dimension_semantics=sem, vmem_limit_bytes=VMEM_LIMIT)


def _dot(a, b):
    return jnp.dot(a, b, preferred_element_type=F32)


def _dot_nt(a, b):
    return lax.dot_general(a, b, (((1,), (1,)), ((), ())), preferred_element_type=F32)


def _split(a):
    hi = a.astype(BF16)
    lo = (a - hi.astype(F32)).astype(BF16)
    return hi, lo


def _dot3(ah, al, bh, bl):
    return _dot(ah, bh) + (_dot(ah, bl) + _dot(al, bh))


def _dot3_nt(ah, al, bh, bl):
    return _dot_nt(ah, bh) + (_dot_nt(ah, bl) + _dot_nt(al, bh))


def _rms(x, g):
    return x * lax.rsqrt(jnp.mean(x * x, axis=-1, keepdims=True) + NORM_EPS) * g


def _rope_kernel(pos_ref, fr_ref, fm_ref, cr_ref, sr_ref, cm_ref, sm_ref):
    pos = pos_ref[0].astype(F32)
    ang_r = pos * fr_ref[0:1, :]
    ang_m = pos * fm_ref[0:1, :]
    cr_ref[0] = jnp.cos(ang_r) * fr_ref[1:2, :] + fr_ref[2:3, :]
    sr_ref[0] = jnp.sin(ang_r) * fr_ref[1:2, :]
    cm_ref[0] = jnp.cos(ang_m)
    sm_ref[0] = jnp.sin(ang_m) * fm_ref[1:2, :]


def _rope_tables(positions):
    B, S = positions.shape
    ts = 512
    inv_r = 1.0 / (ROPE_THETA ** (jnp.arange(0, MLA_ROPE, 2, dtype=F32) / MLA_ROPE))
    inv_m = 1.0 / (ROPE_THETA ** (jnp.arange(0, MOBA_HEAD_DIM, 2, dtype=F32) / MOBA_HEAD_DIM))
    z32 = jnp.zeros((32,), F32)
    z64 = jnp.zeros((64,), F32)
    fr = jnp.stack([jnp.concatenate([z64, inv_r, inv_r, z32]),
                    jnp.concatenate([z64, jnp.ones((32,), F32), z32]),
                    jnp.concatenate([jnp.ones((64,), F32), z32, z32])])
    fr = jnp.concatenate([fr, jnp.zeros((5, LANES), F32)])
    sign = jnp.tile(jnp.concatenate([-jnp.ones((32,), F32), jnp.ones((32,), F32)]), 2)
    fm = jnp.concatenate([jnp.tile(inv_m, 4)[None, :], sign[None, :], jnp.zeros((6, LANES), F32)])
    tab = jax.ShapeDtypeStruct((B, S, LANES), F32)
    row = pl.BlockSpec((1, ts, LANES), lambda b, i: (b, i, 0))
    return pl.pallas_call(
        _rope_kernel,
        grid=(B, S // ts),
        in_specs=[pl.BlockSpec((1, ts, 1), lambda b, i: (b, i, 0)),
                  pl.BlockSpec((8, LANES), lambda b, i: (0, 0)),
                  pl.BlockSpec((8, LANES), lambda b, i: (0, 0))],
        out_specs=[row, row, row, row],
        out_shape=[tab, tab, tab, tab],
        compiler_params=_params("parallel", "parallel"),
        name="rope_tables",
    )(positions.reshape(B, S, 1), fr, fm)


def _attn_prep_kernel(h_ref, g_ref, w1_ref, w2h_ref, w2l_ref, gq_ref, gkv_ref, wq_ref, wqr_ref,
                      wk_ref, wv_ref, cr_ref, sr_ref, cm_ref, sm_ref,
                      qm_ref, km_ref, vm_ref, qb_ref, qf_ref, kf_ref, kb_ref, vb_ref):
    xn = _rms(h_ref[0], g_ref[...])
    xh, xl = _split(xn)
    p1 = _dot(xh, w1_ref[...])
    p2 = _dot3(xh, xl, w2h_ref[...], w2l_ref[...])
    cq, ckv = p1[:, 0:384], p1[:, 384:640]
    kr1, kr2 = p1[:, 640:768], p1[:, 768:896]
    mv = p1[:, 896:1408]
    cr, sr, cm, sm = cr_ref[0], sr_ref[0], cm_ref[0], sm_ref[0]
    nq = _rms(cq, gq_ref[...]).astype(BF16)
    nkv = _rms(ckv, gkv_ref[...]).astype(BF16)
    qa = _dot(nq, wq_ref[...])
    qr = _dot(nq, wqr_ref[...])
    ka = _dot(nkv, wk_ref[...])
    va = _dot(nkv, wv_ref[...])
    kpe = kr1 * cr + kr2 * sr
    mla_scale = (MLA_NOPE + MLA_ROPE) ** -0.5 * LOG2E
    for h in range(MLA_HEADS):
        sl = slice(h * LANES, (h + 1) * LANES)
        qm_ref[0, h] = ((qa[:, sl] * cr + qr[:, sl] * sr) * mla_scale).astype(BF16)
        km_ref[0, h] = (ka[:, sl] + kpe).astype(BF16)
        vm_ref[0, h] = va[:, sl].astype(BF16)
    lane = lax.broadcasted_iota(jnp.int32, cm.shape, 1)
    low = lane < MOBA_HEAD_DIM
    first = (lane & (MOBA_HEAD_DIM - 1)) < MOBA_HEAD_DIM // 2

    def rope(x):
        half = MOBA_HEAD_DIM // 2
        return x * cm + jnp.where(first, pltpu.roll(x, LANES - half, axis=1), pltpu.roll(x, half, axis=1)) * sm

    moba_scale = MOBA_HEAD_DIM ** -0.5 * LOG2E
    for p in range(HEAD_PAIRS):
        sl = slice(p * LANES, (p + 1) * LANES)
        q = rope(p2[:, sl])
        k = rope(p2[:, 512 + p * LANES:512 + (p + 1) * LANES])
        v = mv[:, sl]
        qf_ref[0, :, sl] = q
        kf_ref[0, :, sl] = k
        qb_ref[0, :, sl] = (q * moba_scale).astype(BF16)
        kb_ref[0, 2 * p] = jnp.where(low, k, 0.0).astype(BF16)
        kb_ref[0, 2 * p + 1] = jnp.where(low, 0.0, k).astype(BF16)
        vb_ref[0, 2 * p] = jnp.where(low, v, 0.0).astype(BF16)
        vb_ref[0, 2 * p + 1] = jnp.where(low, 0.0, v).astype(BF16)


def _attn_prep(h, g, w, tabs):
    B, S, D = h.shape
    tm = PROJ_TM
    const = lambda a: pl.BlockSpec(a.shape, lambda b, i: (0,) * a.ndim)
    row = lambda n: pl.BlockSpec((1, tm, n), lambda b, i: (b, i, 0))
    head = pl.BlockSpec((1, 8, tm, LANES), lambda b, i: (b, 0, i, 0))
    head_shape = jax.ShapeDtypeStruct((B, 8, S, LANES), BF16)
    weights = [g, w["w1"], w["w2h"], w["w2l"], w["gq"], w["gkv"], w["wq"], w["wqr"], w["wk"], w["wv"]]
    return pl.pallas_call(
        _attn_prep_kernel,
        grid=(B, S // tm),
        in_specs=[row(D)] + [const(a) for a in weights] + [row(LANES)] * 4,
        out_specs=[head, head, head, row(512), row(512), row(512), head, head],
        out_shape=[head_shape, head_shape, head_shape,
                   jax.ShapeDtypeStruct((B, S, 512), BF16),
                   jax.ShapeDtypeStruct((B, S, 512), F32),
                   jax.ShapeDtypeStruct((B, S, 512), F32),
                   head_shape, head_shape],
        compiler_params=_params("parallel", "parallel"),
        name="attn_prep",
    )(h, *weights, *tabs)


def _moba_gate_kernel(q_ref, k_ref, sel_ref, kmh_ref, kml_ref):
    i = pl.program_id(2)
    tq = q_ref.shape[1]
    S = k_ref.shape[1]
    nb = S // MOBA_BLOCK

    @pl.when(i == 0)
    def _():
        km = jnp.sum(k_ref[0].reshape(nb, MOBA_BLOCK, LANES), axis=1) / float(MOBA_BLOCK)
        low = lax.broadcasted_iota(jnp.int32, km.shape, 1) < MOBA_HEAD_DIM
        for hh, m in enumerate((jnp.where(low, km, 0.0), jnp.where(low, 0.0, km))):
            hi, lo = _split(m)
            kmh_ref[hh] = hi
            kml_ref[hh] = lo

    qh, ql = _split(q_ref[0])
    row = i * tq + lax.broadcasted_iota(jnp.int32, (tq, nb), 0)
    blk = lax.broadcasted_iota(jnp.int32, (tq, nb), 1)
    valid = blk * MOBA_BLOCK < row - (row & (MOBA_BLOCK - 1))
    for hh in range(2):
        gate = _dot3_nt(qh, ql, kmh_ref[hh], kml_ref[hh])
        g = jnp.where(valid, gate, -jnp.inf)
        rest = g
        for _ in range(MOBA_TOPK - 1):
            top = jnp.max(rest, axis=1, keepdims=True)
            rest = jnp.where(rest == top, -jnp.inf, rest)
        thr = jnp.max(rest, axis=1, keepdims=True)
        sel_ref[0, hh] = jnp.where(valid & (g >= thr), 1.0, 0.0)


def _moba_gate(qf, kf):
    B, S, _ = qf.shape
    tq = ATTN_TQ
    nb = S // MOBA_BLOCK
    return pl.pallas_call(
        _moba_gate_kernel,
        grid=(B, HEAD_PAIRS, S // tq),
        in_specs=[pl.BlockSpec((1, tq, LANES), lambda b, p, i: (b, i, p)),
                  pl.BlockSpec((1, S, LANES), lambda b, p, i: (b, 0, p))],
        out_specs=pl.BlockSpec((1, 2, tq, nb), lambda b, p, i: (b, p, i, 0)),
        out_shape=jax.ShapeDtypeStruct((B, 8, S, nb), F32),
        scratch_shapes=[pltpu.VMEM((2, nb, LANES), BF16), pltpu.VMEM((2, nb, LANES), BF16)],
        compiler_params=_params("parallel", "parallel", "arbitrary"),
        name="moba_gate",
    )(qf, kf)


def _flash_kernel(*refs, moba, paired_q):
    if moba:
        q_ref, k_ref, v_ref, sel_ref, o_ref, m_scr, l_scr, acc_scr = refs
    else:
        q_ref, k_ref, v_ref, o_ref, m_scr, l_scr, acc_scr = refs
    i = pl.program_id(2)
    tq, tk = ATTN_TQ, ATTN_TK
    heads = 2 * FLASH_PAIRS
    if paired_q:
        qs = [q_ref[0, :, (hd // 2) * LANES:(hd // 2 + 1) * LANES] for hd in range(heads)]
    else:
        qs = [q_ref[0, hd] for hd in range(heads)]
    m_scr[...] = jnp.full(m_scr.shape, NEG, F32)
    l_scr[...] = jnp.zeros(l_scr.shape, F32)
    acc_scr[...] = jnp.zeros(acc_scr.shape, F32)
    low = lax.broadcasted_iota(jnp.int32, (tq, LANES), 1) < 64

    def step(j, diag):
        start = pl.multiple_of(j * tk, tk)
        for pp in range(FLASH_PAIRS):
            alphas, pvs = [], []
            for hh in range(2):
                hd = 2 * pp + hh
                k = k_ref[0, hd, pl.ds(start, tk), :]
                v = v_ref[0, hd, pl.ds(start, tk), :]
                s = _dot_nt(qs[hd], k)
                if moba:
                    sel = sel_ref[0, hd]
                    col = lax.broadcasted_iota(jnp.int32, sel.shape, 1)
                    sub = tk // MOBA_BLOCK
                    picked = jnp.concatenate(
                        [jnp.broadcast_to(
                            jnp.sum(jnp.where(col == j * sub + r, sel, 0.0), axis=1, keepdims=True),
                            (tq, MOBA_BLOCK)) for r in range(sub)], axis=1) > 0.5
                if diag:
                    qpos = i * tq + lax.broadcasted_iota(jnp.int32, (tq, tk), 0)
                    kpos = j * tk + lax.broadcasted_iota(jnp.int32, (tq, tk), 1)
                    ok = kpos <= qpos
                    if moba:
                        ok = (ok & (kpos >= qpos - (qpos & (MOBA_BLOCK - 1)))) | picked
                    s = jnp.where(ok, s, NEG)
                elif moba:
                    s = jnp.where(picked, s, NEG)
                m_prev = m_scr[hd]
                m_next = jnp.maximum(m_prev, jnp.max(s, axis=1, keepdims=True))
                alpha = jnp.exp2(m_prev - m_next)
                p = jnp.exp2(s - jnp.tile(m_next, (1, tk // LANES)))
                l_scr[hd] = alpha * l_scr[hd] + jnp.sum(p, axis=1, keepdims=True)
                m_scr[hd] = m_next
                alphas.append(alpha)
                pvs.append(_dot(p.astype(BF16), v))
            acc_scr[pp] = acc_scr[pp] * jnp.where(low, alphas[0], alphas[1]) + (pvs[0] + pvs[1])

    n_full = i * (tq // tk)

    def body(j, carry):
        step(j, False)
        return carry

    lax.fori_loop(0, n_full, body, 0)
    for r in range(tq // tk):
        step(n_full + r, True)
    for pp in range(FLASH_PAIRS):
        o_ref[0, :, pp * LANES:(pp + 1) * LANES] = (
            acc_scr[pp] / jnp.where(low, l_scr[2 * pp], l_scr[2 * pp + 1])).astype(o_ref.dtype)


def _flash(q, k, v, sel=None):
    moba = sel is not None
    B, _, S, _ = k.shape
    tq = ATTN_TQ
    np_, heads = FLASH_PAIRS, 2 * FLASH_PAIRS
    kv_spec = pl.BlockSpec((1, heads, S, LANES), lambda b, p, i: (b, p, 0, 0))
    if moba:
        q_spec = pl.BlockSpec((1, tq, np_ * LANES), lambda b, p, i: (b, i, p))
        extra = [pl.BlockSpec((1, heads, tq, sel.shape[3]), lambda b, p, i: (b, p, i, 0))]
        args = (q, k, v, sel)
    else:
        q_spec = pl.BlockSpec((1, heads, tq, LANES), lambda b, p, i: (b, p, i, 0))
        extra = []
        args = (q, k, v)
    return pl.pallas_call(
        functools.partial(_flash_kernel, moba=moba, paired_q=moba),
        grid=(B, HEAD_PAIRS // np_, S // tq),
        in_specs=[q_spec, kv_spec, kv_spec] + extra,
        out_specs=pl.BlockSpec((1, tq, np_ * LANES), lambda b, p, i: (b, i, p)),
        out_shape=jax.ShapeDtypeStruct((B, S, HEAD_PAIRS * LANES), BF16),
        scratch_shapes=[pltpu.VMEM((heads, tq, LANES), F32), pltpu.VMEM((heads, tq, LANES), F32),
                        pltpu.VMEM((np_, tq, LANES), F32)],
        compiler_params=_params("parallel", "parallel", "arbitrary"),
        name="flash_moba" if moba else "flash_mla",
    )(*args)


def _attn_out_kernel(h_ref, oa_ref, ob_ref, w_ref, o_ref):
    half = oa_ref.shape[1]
    o_ref[...] = h_ref[...] + (_dot(oa_ref[...], w_ref[0:half, :]) + _dot(ob_ref[...], w_ref[half:, :]))


def _attn_out(h2, o_mla, o_moba, w_out):
    T, D = h2.shape
    tm = 512
    half = o_mla.shape[1]
    return pl.pallas_call(
        _attn_out_kernel,
        grid=(T // tm,),
        in_specs=[pl.BlockSpec((tm, D), lambda i: (i, 0)),
                  pl.BlockSpec((tm, half), lambda i: (i, 0)),
                  pl.BlockSpec((tm, half), lambda i: (i, 0)),
                  pl.BlockSpec(w_out.shape, lambda i: (0, 0))],
        out_specs=pl.BlockSpec((tm, D), lambda i: (i, 0)),
        out_shape=jax.ShapeDtypeStruct((T, D), F32),
        compiler_params=_params("parallel"),
        name="attn_out",
    )(h2, o_mla, o_moba, w_out)


def _attn_weights(w_in, q_norm, w_uq, kv_norm, w_ukv, w_out):
    D = w_in.shape[0]
    w_cq, w_ckv = w_in[:, 0:384], w_in[:, 384:640]
    w_kr = w_in[:, 640:672]
    w_mq, w_mk, w_mv = w_in[:, 672:1184], w_in[:, 1184:1696], w_in[:, 1696:2208]
    z = lambda n: jnp.zeros((D, n), F32)
    kr1 = jnp.concatenate([z(64), w_kr, z(32)], axis=1)
    kr2 = jnp.concatenate([z(64), -w_kr[:, 16:32], w_kr[:, 0:16], z(32)], axis=1)
    w1 = jnp.concatenate([w_cq, w_ckv, kr1, kr2, w_mv], axis=1).astype(BF16)

    w2 = jnp.concatenate([w_mq, w_mk], axis=1)
    w2h = w2.astype(BF16)
    w2l = (w2 - w2h.astype(F32)).astype(BF16)

    uq = w_uq.reshape(MLA_Q_LORA, MLA_HEADS, MLA_NOPE + MLA_ROPE)
    nope, pe1, pe2 = uq[..., :64], uq[..., 64:80], uq[..., 80:96]
    zq = lambda n: jnp.zeros((MLA_Q_LORA, MLA_HEADS, n), F32)
    wq = jnp.concatenate([nope, pe1, pe2, zq(32)], axis=-1).reshape(MLA_Q_LORA, 1024).astype(BF16)
    wqr = jnp.concatenate([zq(64), -pe2, pe1, zq(32)], axis=-1).reshape(MLA_Q_LORA, 1024).astype(BF16)

    ukv = w_ukv.reshape(MLA_KV_LORA, MLA_HEADS, MLA_NOPE + MLA_V)
    k_nope, v = ukv[..., :64], ukv[..., 64:]
    zk = jnp.zeros((MLA_KV_LORA, MLA_HEADS, 64), F32)
    wk = jnp.concatenate([k_nope, zk], axis=-1).reshape(MLA_KV_LORA, 1024).astype(BF16)
    even = (jnp.arange(MLA_HEADS) % 2 == 0)[None, :, None]
    wv = jnp.concatenate([jnp.where(even, v, 0.0), jnp.where(even, 0.0, v)], axis=-1)
    wv = wv.reshape(MLA_KV_LORA, 1024).astype(BF16)
    return dict(w1=w1, w2h=w2h, w2l=w2l, gq=q_norm[None, :], gkv=kv_norm[None, :], wq=wq, wqr=wqr,
                wk=wk, wv=wv, w_out=w_out.astype(BF16))


def _attention_layer(h, g, w, tabs):
    B, S, D = h.shape
    qm, km, vm, qb, qf, kf, kb, vb = _attn_prep(h, g[None, :], w, tabs)
    o_mla = _flash(qm, km, vm)
    sel = _moba_gate(qf, kf)
    o_moba = _flash(qb, kb, vb, sel)
    out = _attn_out(h.reshape(B * S, D), o_mla.reshape(B * S, -1), o_moba.reshape(B * S, -1), w["w_out"])
    return out.reshape(B, S, D)


def _conv_in_kernel(h_ref, g_ref, wu_ref, wg_ref, bu_ref, bg_ref, o_ref):
    xn = _rms(h_ref[...], g_ref[...]).astype(BF16)
    u = _dot(xn, wu_ref[...]) + bu_ref[...]
    gate = _dot(xn, wg_ref[...]) + bg_ref[...]
    o_ref[...] = u * jax.nn.sigmoid(gate)


def _conv_in(h2, g, w_in, b_in):
    T, D = h2.shape
    tm = 512
    const = lambda a: pl.BlockSpec(a.shape, lambda i: (0, 0))
    args = [g[None, :], w_in[:, :D].astype(BF16), w_in[:, D:].astype(BF16), b_in[None, :D], b_in[None, D:]]
    return pl.pallas_call(
        _conv_in_kernel,
        grid=(T // tm,),
        in_specs=[pl.BlockSpec((tm, D), lambda i: (i, 0))] + [const(a) for a in args],
        out_specs=pl.BlockSpec((tm, D), lambda i: (i, 0)),
        out_shape=jax.ShapeDtypeStruct((T, D), F32),
        compiler_params=_params("parallel"),
        name="conv_in",
    )(h2, *args)


def _conv_main_kernel(u_ref, up_ref, h_ref, wdw_ref, bdw_ref, lg_ref, lb_ref, wo_ref, bo_ref, o_ref,
                      buf_scr, y_scr):
    i = pl.program_id(1)
    ts = u_ref.shape[1]
    rows = CONV_HALO + ts
    buf_scr[0, 0:CONV_HALO, :] = jnp.where(i == 0, 0.0, up_ref[0])
    buf_scr[0, CONV_HALO:rows, :] = u_ref[0]
    for ph in range(1, SUBLANES):
        buf_scr[ph] = pltpu.roll(buf_scr[0], rows - ph, axis=0)
    lead = CONV_HALO - (CONV_WIDTH - 1)
    for r in range(ts // CONV_ROWS):
        acc = jnp.broadcast_to(bdw_ref[...], (CONV_ROWS, bdw_ref.shape[1]))
        for w in range(CONV_WIDTH):
            ph = (lead + w) % SUBLANES
            base = r * CONV_ROWS + (lead + w) - ph
            acc = acc + buf_scr[ph, base:base + CONV_ROWS, :] * jnp.tile(wdw_ref[w], (CONV_ROWS // SUBLANES, 1))
        mu = jnp.mean(acc, axis=-1, keepdims=True)
        cen = acc - mu
        var = jnp.mean(cen * cen, axis=-1, keepdims=True)
        y = cen * lax.rsqrt(var + NORM_EPS) * lg_ref[...] + lb_ref[...]
        y_scr[r * CONV_ROWS:(r + 1) * CONV_ROWS, :] = (y * jax.nn.sigmoid(y)).astype(BF16)
    o_ref[0] = h_ref[0] + (_dot(y_scr[...], wo_ref[...]) + bo_ref[...])


def _conv_main(u, h, w_dw, b_dw, ln_g, ln_b, w_out, b_out):
    B, S, D = h.shape
    ts = CONV_TS
    per = ts // CONV_HALO
    wdw = jnp.broadcast_to(w_dw[:, None, :], (CONV_WIDTH, SUBLANES, D))
    const = lambda a: pl.BlockSpec(a.shape, lambda b, i: (0,) * a.ndim)
    args = [wdw, b_dw[None, :], ln_g[None, :], ln_b[None, :], w_out.astype(BF16), b_out[None, :]]
    tile = pl.BlockSpec((1, ts, D), lambda b, i: (b, i, 0))
    return pl.pallas_call(
        _conv_main_kernel,
        grid=(B, S // ts),
        in_specs=[tile,
                  pl.BlockSpec((1, CONV_HALO, D), lambda b, i: (b, jnp.maximum(i * per - 1, 0), 0)),
                  tile] + [const(a) for a in args],
        out_specs=tile,
        out_shape=jax.ShapeDtypeStruct((B, S, D), F32),
        scratch_shapes=[pltpu.VMEM((SUBLANES, CONV_HALO + ts, D), F32), pltpu.VMEM((ts, D), BF16)],
        compiler_params=_params("parallel", "arbitrary"),
        name="conv_main",
    )(u, u, h, *args)


def _conv_layer(h, g, w_in, b_in, w_dw, b_dw, ln_g, ln_b, w_out, b_out):
    B, S, D = h.shape
    u = _conv_in(h.reshape(B * S, D), g, w_in, b_in).reshape(B, S, D)
    return _conv_main(u, h, w_dw, b_dw, ln_g, ln_b, w_out, b_out)


def _sort16_network():
    def merge(lo, hi, r):
        step = r * 2
        if step < hi - lo:
            yield from merge(lo, hi, step)
            yield from merge(lo + r, hi, step)
            yield from [(i, i + r) for i in range(lo + r, hi - r, step)]
        else:
            yield (lo, lo + r)

    def sort(lo, hi):
        if hi - lo >= 1:
            mid = lo + (hi - lo) // 2
            yield from sort(lo, mid)
            yield from sort(mid + 1, hi)
            yield from merge(lo, hi, 1)

    return tuple(sort(0, PEER_TOPK - 1))


def _top16_sorted(sc):
    g = [sc[SUBLANES * i:SUBLANES * (i + 1), :] for i in range(PEER_TOPK)]
    for i, j in _sort16_network():
        g[i], g[j] = jnp.maximum(g[i], g[j]), jnp.minimum(g[i], g[j])
    for shift in (4, 2, 1):
        y = [pltpu.roll(x, shift, axis=0) for x in g]
        z = [jnp.maximum(g[i], y[PEER_TOPK - 1 - i]) for i in range(PEER_TOPK)]
        for d in (8, 4, 2, 1):
            for i in range(PEER_TOPK):
                if i & d == 0:
                    z[i], z[i + d] = jnp.maximum(z[i], z[i + d]), jnp.minimum(z[i], z[i + d])
        g = z
    return g


def _dup_bf16(x):
    bits = pltpu.bitcast(x.astype(BF16).astype(F32), jnp.uint32)
    return (bits & jnp.uint32(0xFFFF0000)) | (bits >> 16)


def _peer_route_kernel(h_ref, g_ref, wqh_ref, wql_ref, skh_ref, skl_ref,
                       xn_ref, cnt_ref, e1_ref, rank_ref, e2_ref, q_scr, top_scr):
    xn = _rms(h_ref[...], g_ref[...])
    xh, xl = _split(xn)
    xn_ref[...] = xh
    q = _dot3(xh, xl, wqh_ref[...], wql_ref[...])
    for hc in range(2 * PEER_HEADS):
        q_scr[hc] = q[:, hc * PEER_D_HALF:(hc + 1) * PEER_D_HALF]

    def head(h, carry):
        s = []
        rank = None
        for c in range(2):
            qh, ql = _split(q_scr[2 * h + c])
            sc = _dot3_nt(skh_ref[2 * h + c], skl_ref[2 * h + c], qh, ql)
            s.append(sc)
            tops = _top16_sorted(sc)
            for k in range(PEER_TOPK):
                top_scr[c, k:k + 1, :] = tops[k][0:1, :]
            if c == 1:
                rank = jnp.full(sc.shape, float(PEER_TOPK), F32)
                grouped = sc.reshape(PEER_N_KEYS // SUBLANES, SUBLANES, sc.shape[1])
                rank = rank.reshape(grouped.shape)
                for k in reversed(range(PEER_TOPK)):
                    rank = jnp.where(grouped >= tops[k][None], float(k), rank)
                rank = rank.reshape(sc.shape)
        a = top_scr[0]
        b = top_scr[1]
        cands = [a[0:1] + b, a[1:2] + b[0:8]]
        cands += [a[i:i + 1] + b[0:8] for i in range(2, 8)]
        cands += [a[8:16] + b[0:1]]
        m_top = a[0:1] + b[0:1]
        pad = [jnp.full(cands[1].shape, -jnp.inf, F32)] * (PEER_TOPK - len(cands) - 1)
        best = _top16_sorted(jnp.concatenate(cands + pad, axis=0))
        thr = best[PEER_TOPK - 1][0:1, :]
        z = jnp.zeros_like(m_top)
        for k in range(PEER_TOPK):
            z = z + jnp.exp(best[k][0:1, :] - m_top)
        cnt = jnp.zeros(s[0].shape, F32)
        for j in range(PEER_TOPK):
            cnt = cnt + jnp.where(s[0] + b[j:j + 1] >= thr, 1.0, 0.0)
        cnt_ref[h] = _dup_bf16(cnt)
        e1_ref[h] = _dup_bf16(jnp.exp(s[0] - a[0:1]) * (0.5 / z))
        rank_ref[h] = rank.astype(BF16)
        e2_ref[h] = jnp.exp(s[1] - b[0:1]).astype(BF16)
        return carry

    lax.fori_loop(0, PEER_HEADS, head, 0)


def _peer_route(h2, g, wqh, wql, skh, skl):
    T, D = h2.shape
    tm = PEER_ROUTE_TM
    const = lambda a: pl.BlockSpec(a.shape, lambda i: (0,) * a.ndim)
    stat = pl.BlockSpec((PEER_HEADS, PEER_N_KEYS, tm), lambda i: (0, 0, i))
    a_shape = jax.ShapeDtypeStruct((PEER_HEADS, PEER_N_KEYS, T), jnp.uint32)
    b_shape = jax.ShapeDtypeStruct((PEER_HEADS, PEER_N_KEYS, T), BF16)
    args = [g[None, :], wqh, wql, skh, skl]
    return pl.pallas_call(
        _peer_route_kernel,
        grid=(T // tm,),
        in_specs=[pl.BlockSpec((tm, D), lambda i: (i, 0))] + [const(a) for a in args],
        out_specs=[pl.BlockSpec((tm, D), lambda i: (i, 0)), stat, stat, stat, stat],
        out_shape=[jax.ShapeDtypeStruct((T, D), BF16), a_shape, a_shape, b_shape, b_shape],
        scratch_shapes=[pltpu.VMEM((2 * PEER_HEADS, tm, PEER_D_HALF), F32),
                        pltpu.VMEM((2, PEER_TOPK, tm), F32)],
        compiler_params=_params("parallel"),
        name="peer_route",
    )(h2, *args)


def _gelu_x2(x):
    return x + x * lax.erf(x * (2.0 ** -0.5))


def _bcast_row(ref, h, a):
    row = ref[h, a:a + 1, :]
    return pltpu.bitcast(jnp.broadcast_to(row, (8, row.shape[1])), BF16)


def _peer_main_kernel(xn_ref, u_ref, vt_ref, cnt_ref, e1_ref, rank_ref, e2_ref, h_ref, o_ref, acc_scr, w_scr):
    k = pl.program_id(1)
    tm = w_scr.shape[1]
    pair = 2 * PEER_N_KEYS

    @pl.when(k == 0)
    def _():
        acc_scr[...] = jnp.zeros(acc_scr.shape, F32)

    for p in range(PEER_A_TILE // 2):
        prow = slice(p * pair, (p + 1) * pair)
        st = _dot_nt(u_ref[prow, :], xn_ref[...])
        for r in range(2):
            a = 2 * p + r
            gsum = None
            for h in range(PEER_HEADS):
                cnt = _bcast_row(cnt_ref, h, a)
                e1 = _bcast_row(e1_ref, h, a)
                reps = (PEER_N_KEYS // cnt.shape[0], 1)
                picked = rank_ref[h] < jnp.tile(cnt, reps)
                term = jnp.where(picked, e2_ref[h] * jnp.tile(e1, reps), jnp.zeros((), BF16))
                gsum = term if gsum is None else gsum + term
            act = _gelu_x2(st[r * PEER_N_KEYS:(r + 1) * PEER_N_KEYS, :].astype(BF16))
            w_scr[a * PEER_N_KEYS:(a + 1) * PEER_N_KEYS, :] = gsum * act
    acc_scr[...] += _dot(vt_ref[...], w_scr[...])

    @pl.when(k == pl.num_programs(1) - 1)
    def _():
        o_ref[...] = h_ref[...] + acc_scr[...].T


def _peer_main(h2, xn, u_bf, vt_bf, cnt, e1, rank, e2):
    T, D = h2.shape
    E = u_bf.shape[0]
    tm = PEER_TM
    te = PEER_A_TILE * PEER_N_KEYS
    tok = pl.BlockSpec((tm, D), lambda i, k: (i, 0))
    a_stat = pl.BlockSpec((PEER_HEADS, PEER_A_TILE, tm), lambda i, k: (0, k, i))
    b_stat = pl.BlockSpec((PEER_HEADS, PEER_N_KEYS, tm), lambda i, k: (0, 0, i))
    return pl.pallas_call(
        _peer_main_kernel,
        grid=(T // tm, E // te),
        in_specs=[tok,
                  pl.BlockSpec((te, D), lambda i, k: (k, 0)),
                  pl.BlockSpec((D, te), lambda i, k: (0, k)),
                  a_stat, a_stat, b_stat, b_stat, tok],
        out_specs=tok,
        out_shape=jax.ShapeDtypeStruct((T, D), F32),
        scratch_shapes=[pltpu.VMEM((D, tm), F32), pltpu.VMEM((te, tm), BF16)],
        compiler_params=_params("parallel", "arbitrary"),
        name="peer_main",
    )(xn, u_bf, vt_bf, cnt, e1, rank, e2, h2)


def _peer_layer(h, g, w_q, subkeys, u_tab, v_tab):
    B, S, D = h.shape
    h2 = h.reshape(B * S, D)
    wqh = w_q.astype(BF16)
    wql = (w_q - wqh.astype(F32)).astype(BF16)
    sk = subkeys.reshape(2 * PEER_HEADS, PEER_N_KEYS, PEER_D_HALF)
    skh = sk.astype(BF16)
    skl = (sk - skh.astype(F32)).astype(BF16)
    xn, cnt, e1, rank, e2 = _peer_route(h2, g, wqh, wql, skh, skl)
    out = _peer_main(h2, xn, u_tab.astype(BF16), v_tab.T.astype(BF16), cnt, e1, rank, e2)
    return out.reshape(B, S, D)


def _final_norm_kernel(h_ref, g_ref, o_ref):
    o_ref[...] = _rms(h_ref[...], g_ref[...])


def _final_norm(h2, g):
    T, D = h2.shape
    tm = 512
    return pl.pallas_call(
        _final_norm_kernel,
        grid=(T // tm,),
        in_specs=[pl.BlockSpec((tm, D), lambda i: (i, 0)), pl.BlockSpec((1, D), lambda i: (0, 0))],
        out_specs=pl.BlockSpec((tm, D), lambda i: (i, 0)),
        out_shape=jax.ShapeDtypeStruct((T, D), F32),
        compiler_params=_params("parallel"),
        name="final_norm",
    )(h2, g[None, :])


def kernel(x, positions, norm_mix, norm_ffn, norm_final, attn_w_in, mla_q_norm, mla_w_uq, mla_kv_norm,
           mla_w_ukv, attn_w_out, conv_w_in, conv_b_in, conv_w_dw, conv_b_dw, conv_ln_g, conv_ln_b,
           conv_w_out, conv_b_out, peer_w_q, peer_subkeys, peer_u, peer_v):
    B, S, D = x.shape
    assert D == D_MODEL and S % ATTN_TQ == 0 and S % MOBA_BLOCK == 0 and (B * S) % PEER_TM == 0
    depth = norm_mix.shape[0]
    tabs = _rope_tables(positions)
    h = x
    for layer in range(depth):
        i = layer // 2
        if layer % 2 == 0:
            w = _attn_weights(attn_w_in[i], mla_q_norm[i], mla_w_uq[i], mla_kv_norm[i], mla_w_ukv[i],
                              attn_w_out[i])
            h = _attention_layer(h, norm_mix[layer], w, tabs)
        else:
            h = _conv_layer(h, norm_mix[layer], conv_w_in[i], conv_b_in[i], conv_w_dw[i], conv_b_dw[i],
                            conv_ln_g[i], conv_ln_b[i], conv_w_out[i], conv_b_out[i])
        h = _peer_layer(h, norm_ffn[layer], peer_w_q[layer], peer_subkeys[layer], peer_u[layer],
                        peer_v[layer])
    return _final_norm(h.reshape(B * S, D), norm_final).reshape(B, S, D)
```

```python
import functools

import jax
import jax.numpy as jnp
from jax import lax
from jax.experimental import pallas as pl
from jax.experimental.pallas import tpu as pltpu

F32 = jnp.float32
BF16 = jnp.bfloat16

D_MODEL = 1024
ROPE_THETA = 10000.0
NORM_EPS = 1e-6

MLA_HEADS = 8
MLA_Q_LORA = 384
MLA_KV_LORA = 256
MLA_NOPE = 64
MLA_ROPE = 32
MLA_V = 64

MOBA_HEADS = 8
MOBA_HEAD_DIM = 64
MOBA_BLOCK = 256
MOBA_TOPK = 3

CONV_WIDTH = 31

PEER_HEADS = 8
PEER_N_KEYS = 128
PEER_D_HALF = 128
PEER_TOPK = 16

LANES = 128
SUBLANES = 8
HEAD_PAIRS = 4
VMEM_LIMIT = 56 * 1024 * 1024
NEG = -1e30
LOG2E = 1.4426950408889634

ATTN_TQ = 512
ATTN_TK = 512
FLASH_PAIRS = 2
PROJ_TM = 256
PEER_ROUTE_TM = 256
PEER_TM = 512
PEER_A_TILE = 32
CONV_TS = 512
CONV_HALO = 32
CONV_ROWS = 32


def _params(*sem):
    return pltpu.CompilerParams(dimension_semantics=sem, vmem_limit_bytes=VMEM_LIMIT)


def _dot(a, b):
    return jnp.dot(a, b, preferred_element_type=F32)


def _dot_nt(a, b):
    return lax.dot_general(a, b, (((1,), (1,)), ((), ())), preferred_element_type=F32)


def _split(a):
    hi = a.astype(BF16)
    lo = (a - hi.astype(F32)).astype(BF16)
    return hi, lo


def _dot3(ah, al, bh, bl):
    return _dot(ah, bh) + (_dot(ah, bl) + _dot(al, bh))


def _dot3_nt(ah, al, bh, bl):
    return _dot_nt(ah, bh) + (_dot_nt(ah, bl) + _dot_nt(al, bh))


def _rms(x, g):
    return x * lax.rsqrt(jnp.mean(x * x, axis=-1, keepdims=True) + NORM_EPS) * g


def _rope_kernel(pos_ref, fr_ref, fm_ref, cr_ref, sr_ref, cm_ref, sm_ref):
    pos = pos_ref[0].astype(F32)
    ang_r = pos * fr_ref[0:1, :]
    ang_m = pos * fm_ref[0:1, :]
    cr_ref[0] = jnp.cos(ang_r) * fr_ref[1:2, :] + fr_ref[2:3, :]
    sr_ref[0] = jnp.sin(ang_r) * fr_ref[1:2, :]
    cm_ref[0] = jnp.cos(ang_m)
    sm_ref[0] = jnp.sin(ang_m) * fm_ref[1:2, :]


def _rope_tables(positions):
    B, S = positions.shape
    ts = 512
    inv_r = 1.0 / (ROPE_THETA ** (jnp.arange(0, MLA_ROPE, 2, dtype=F32) / MLA_ROPE))
    inv_m = 1.0 / (ROPE_THETA ** (jnp.arange(0, MOBA_HEAD_DIM, 2, dtype=F32) / MOBA_HEAD_DIM))
    z32 = jnp.zeros((32,), F32)
    z64 = jnp.zeros((64,), F32)
    fr = jnp.stack([jnp.concatenate([z64, inv_r, inv_r, z32]),
                    jnp.concatenate([z64, jnp.ones((32,), F32), z32]),
                    jnp.concatenate([jnp.ones((64,), F32), z32, z32])])
    fr = jnp.concatenate([fr, jnp.zeros((5, LANES), F32)])
    sign = jnp.tile(jnp.concatenate([-jnp.ones((32,), F32), jnp.ones((32,), F32)]), 2)
    fm = jnp.concatenate([jnp.tile(inv_m, 4)[None, :], sign[None, :], jnp.zeros((6, LANES), F32)])
    tab = jax.ShapeDtypeStruct((B, S, LANES), F32)
    row = pl.BlockSpec((1, ts, LANES), lambda b, i: (b, i, 0))
    return pl.pallas_call(
        _rope_kernel,
        grid=(B, S // ts),
        in_specs=[pl.BlockSpec((1, ts, 1), lambda b, i: (b, i, 0)),
                  pl.BlockSpec((8, LANES), lambda b, i: (0, 0)),
                  pl.BlockSpec((8, LANES), lambda b, i: (0, 0))],
        out_specs=[row, row, row, row],
        out_shape=[tab, tab, tab, tab],
        compiler_params=_params("parallel", "parallel"),
        name="rope_tables",
    )(positions.reshape(B, S, 1), fr, fm)


def _attn_prep_kernel(h_ref, g_ref, w1_ref, w2h_ref, w2l_ref, gq_ref, gkv_ref, wq_ref, wqr_ref,
                      wk_ref, wv_ref, cr_ref, sr_ref, cm_ref, sm_ref,
                      qm_ref, km_ref, vm_ref, qb_ref, qf_ref, kf_ref, kb_ref, vb_ref):
    xn = _rms(h_ref[0], g_ref[...])
    xh, xl = _split(xn)
    p1 = _dot(xh, w1_ref[...])
    p2 = _dot3(xh, xl, w2h_ref[...], w2l_ref[...])
    cq, ckv = p1[:, 0:384], p1[:, 384:640]
    kr1, kr2 = p1[:, 640:768], p1[:, 768:896]
    mv = p1[:, 896:1408]
    cr, sr, cm, sm = cr_ref[0], sr_ref[0], cm_ref[0], sm_ref[0]
    nq = _rms(cq, gq_ref[...]).astype(BF16)
    nkv = _rms(ckv, gkv_ref[...]).astype(BF16)
    qa = _dot(nq, wq_ref[...])
    qr = _dot(nq, wqr_ref[...])
    ka = _dot(nkv, wk_ref[...])
    va = _dot(nkv, wv_ref[...])
    kpe = kr1 * cr + kr2 * sr
    mla_scale = (MLA_NOPE + MLA_ROPE) ** -0.5 * LOG2E
    for h in range(MLA_HEADS):
        sl = slice(h * LANES, (h + 1) * LANES)
        qm_ref[0, h] = ((qa[:, sl] * cr + qr[:, sl] * sr) * mla_scale).astype(BF16)
        km_ref[0, h] = (ka[:, sl] + kpe).astype(BF16)
        vm_ref[0, h] = va[:, sl].astype(BF16)
    lane = lax.broadcasted_iota(jnp.int32, cm.shape, 1)
    low = lane < MOBA_HEAD_DIM
    first = (lane & (MOBA_HEAD_DIM - 1)) < MOBA_HEAD_DIM // 2

    def rope(x):
        half = MOBA_HEAD_DIM // 2
        return x * cm + jnp.where(first, pltpu.roll(x, LANES - half, axis=1), pltpu.roll(x, half, axis=1)) * sm

    moba_scale = MOBA_HEAD_DIM ** -0.5 * LOG2E
    for p in range(HEAD_PAIRS):
        sl = slice(p * LANES, (p + 1) * LANES)
        q = rope(p2[:, sl])
        k = rope(p2[:, 512 + p * LANES:512 + (p + 1) * LANES])
        v = mv[:, sl]
        qf_ref[0, :, sl] = q
        kf_ref[0, :, sl] = k
        qb_ref[0, :, sl] = (q * moba_scale).astype(BF16)
        kb_ref[0, 2 * p] = jnp.where(low, k, 0.0).astype(BF16)
        kb_ref[0, 2 * p + 1] = jnp.where(low, 0.0, k).astype(BF16)
        vb_ref[0, 2 * p] = jnp.where(low, v, 0.0).astype(BF16)
        vb_ref[0, 2 * p + 1] = jnp.where(low, 0.0, v).astype(BF16)


def _attn_prep(h, g, w, tabs):
    B, S, D = h.shape
    tm = PROJ_TM
    const = lambda a: pl.BlockSpec(a.shape, lambda b, i: (0,) * a.ndim)
    row = lambda n: pl.BlockSpec((1, tm, n), lambda b, i: (b, i, 0))
    head = pl.BlockSpec((1, 8, tm, LANES), lambda b, i: (b, 0, i, 0))
    head_shape = jax.ShapeDtypeStruct((B, 8, S, LANES), BF16)
    weights = [g, w["w1"], w["w2h"], w["w2l"], w["gq"], w["gkv"], w["wq"], w["wqr"], w["wk"], w["wv"]]
    return pl.pallas_call(
        _attn_prep_kernel,
        grid=(B, S // tm),
        in_specs=[row(D)] + [const(a) for a in weights] + [row(LANES)] * 4,
        out_specs=[head, head, head, row(512), row(512), row(512), head, head],
        out_shape=[head_shape, head_shape, head_shape,
                   jax.ShapeDtypeStruct((B, S, 512), BF16),
                   jax.ShapeDtypeStruct((B, S, 512), F32),
                   jax.ShapeDtypeStruct((B, S, 512), F32),
                   head_shape, head_shape],
        compiler_params=_params("parallel", "parallel"),
        name="attn_prep",
    )(h, *weights, *tabs)


def _moba_gate_kernel(q_ref, k_ref, sel_ref, kmh_ref, kml_ref):
    i = pl.program_id(2)
    tq = q_ref.shape[1]
    S = k_ref.shape[1]
    nb = S // MOBA_BLOCK

    @pl.when(i == 0)
    def _():
        km = jnp.sum(k_ref[0].reshape(nb, MOBA_BLOCK, LANES), axis=1) / float(MOBA_BLOCK)
        low = lax.broadcasted_iota(jnp.int32, km.shape, 1) < MOBA_HEAD_DIM
        for hh, m in enumerate((jnp.where(low, km, 0.0), jnp.where(low, 0.0, km))):
            hi, lo = _split(m)
            kmh_ref[hh] = hi
            kml_ref[hh] = lo

    qh, ql = _split(q_ref[0])
    row = i * tq + lax.broadcasted_iota(jnp.int32, (tq, nb), 0)
    blk = lax.broadcasted_iota(jnp.int32, (tq, nb), 1)
    valid = blk * MOBA_BLOCK < row - (row & (MOBA_BLOCK - 1))
    for hh in range(2):
        gate = _dot3_nt(qh, ql, kmh_ref[hh], kml_ref[hh])
        g = jnp.where(valid, gate, -jnp.inf)
        rest = g
        for _ in range(MOBA_TOPK - 1):
            top = jnp.max(rest, axis=1, keepdims=True)
            rest = jnp.where(rest == top, -jnp.inf, rest)
        thr = jnp.max(rest, axis=1, keepdims=True)
        sel_ref[0, hh] = jnp.where(valid & (g >= thr), 1.0, 0.0)


def _moba_gate(qf, kf):
    B, S, _ = qf.shape
    tq = ATTN_TQ
    nb = S // MOBA_BLOCK
    return pl.pallas_call(
        _moba_gate_kernel,
        grid=(B, HEAD_PAIRS, S // tq),
        in_specs=[pl.BlockSpec((1, tq, LANES), lambda b, p, i: (b, i, p)),
                  pl.BlockSpec((1, S, LANES), lambda b, p, i: (b, 0, p))],
        out_specs=pl.BlockSpec((1, 2, tq, nb), lambda b, p, i: (b, p, i, 0)),
        out_shape=jax.ShapeDtypeStruct((B, 8, S, nb), F32),
        scratch_shapes=[pltpu.VMEM((2, nb, LANES), BF16), pltpu.VMEM((2, nb, LANES), BF16)],
        compiler_params=_params("parallel", "parallel", "arbitrary"),
        name="moba_gate",
    )(qf, kf)


def _flash_kernel(*refs, moba, paired_q):
    if moba:
        q_ref, k_ref, v_ref, sel_ref, o_ref, m_scr, l_scr, acc_scr = refs
    else:
        q_ref, k_ref, v_ref, o_ref, m_scr, l_scr, acc_scr = refs
    i = pl.program_id(2)
    tq, tk = ATTN_TQ, ATTN_TK
    heads = 2 * FLASH_PAIRS
    if paired_q:
        qs = [q_ref[0, :, (hd // 2) * LANES:(hd // 2 + 1) * LANES] for hd in range(heads)]
    else:
        qs = [q_ref[0, hd] for hd in range(heads)]
    m_scr[...] = jnp.full(m_scr.shape, NEG, F32)
    l_scr[...] = jnp.zeros(l_scr.shape, F32)
    acc_scr[...] = jnp.zeros(acc_scr.shape, F32)
    low = lax.broadcasted_iota(jnp.int32, (tq, LANES), 1) < 64

    def step(j, diag):
        start = pl.multiple_of(j * tk, tk)
        for pp in range(FLASH_PAIRS):
            alphas, pvs = [], []
            for hh in range(2):
                hd = 2 * pp + hh
                k = k_ref[0, hd, pl.ds(start, tk), :]
                v = v_ref[0, hd, pl.ds(start, tk), :]
                s = _dot_nt(qs[hd], k)
                if moba:
                    sel = sel_ref[0, hd]
                    col = lax.broadcasted_iota(jnp.int32, sel.shape, 1)
                    sub = tk // MOBA_BLOCK
                    picked = jnp.concatenate(
                        [jnp.broadcast_to(
                            jnp.sum(jnp.where(col == j * sub + r, sel, 0.0), axis=1, keepdims=True),
                            (tq, MOBA_BLOCK)) for r in range(sub)], axis=1) > 0.5
                if diag:
                    qpos = i * tq + lax.broadcasted_iota(jnp.int32, (tq, tk), 0)
                    kpos = j * tk + lax.broadcasted_iota(jnp.int32, (tq, tk), 1)
                    ok = kpos <= qpos
                    if moba:
                        ok = (ok & (kpos >= qpos - (qpos & (MOBA_BLOCK - 1)))) | picked
                    s = jnp.where(ok, s, NEG)
                elif moba:
                    s = jnp.where(picked, s, NEG)
                m_prev = m_scr[hd]
                m_next = jnp.maximum(m_prev, jnp.max(s, axis=1, keepdims=True))
                alpha = jnp.exp2(m_prev - m_next)
                p = jnp.exp2(s - jnp.tile(m_next, (1, tk // LANES)))
                l_scr[hd] = alpha * l_scr[hd] + jnp.sum(p, axis=1, keepdims=True)
                m_scr[hd] = m_next
                alphas.append(alpha)
                pvs.append(_dot(p.astype(BF16), v))
            acc_scr[pp] = acc_scr[pp] * jnp.where(low, alphas[0], alphas[1]) + (pvs[0] + pvs[1])

    n_full = i * (tq // tk)

    def body(j, carry):
        step(j, False)
        return carry

    lax.fori_loop(0, n_full, body, 0)
    for r in range(tq // tk):
        step(n_full + r, True)
    for pp in range(FLASH_PAIRS):
        o_ref[0, :, pp * LANES:(pp + 1) * LANES] = (
            acc_scr[pp] / jnp.where(low, l_scr[2 * pp], l_scr[2 * pp + 1])).astype(o_ref.dtype)


def _flash(q, k, v, sel=None):
    moba = sel is not None
    B, _, S, _ = k.shape
    tq = ATTN_TQ
    np_, heads = FLASH_PAIRS, 2 * FLASH_PAIRS
    kv_spec = pl.BlockSpec((1, heads, S, LANES), lambda b, p, i: (b, p, 0, 0))
    if moba:
        q_spec = pl.BlockSpec((1, tq, np_ * LANES), lambda b, p, i: (b, i, p))
        extra = [pl.BlockSpec((1, heads, tq, sel.shape[3]), lambda b, p, i: (b, p, i, 0))]
        args = (q, k, v, sel)
    else:
        q_spec = pl.BlockSpec((1, heads, tq, LANES), lambda b, p, i: (b, p, i, 0))
        extra = []
        args = (q, k, v)
    return pl.pallas_call(
        functools.partial(_flash_kernel, moba=moba, paired_q=moba),
        grid=(B, HEAD_PAIRS // np_, S // tq),
        in_specs=[q_spec, kv_spec, kv_spec] + extra,
        out_specs=pl.BlockSpec((1, tq, np_ * LANES), lambda b, p, i: (b, i, p)),
        out_shape=jax.ShapeDtypeStruct((B, S, HEAD_PAIRS * LANES), BF16),
        scratch_shapes=[pltpu.VMEM((heads, tq, LANES), F32), pltpu.VMEM((heads, tq, LANES), F32),
                        pltpu.VMEM((np_, tq, LANES), F32)],
        compiler_params=_params("parallel", "parallel", "arbitrary"),
        name="flash_moba" if moba else "flash_mla",
    )(*args)


def _attn_out_kernel(h_ref, oa_ref, ob_ref, w_ref, o_ref):
    half = oa_ref.shape[1]
    o_ref[...] = h_ref[...] + (_dot(oa_ref[...], w_ref[0:half, :]) + _dot(ob_ref[...], w_ref[half:, :]))


def _attn_out(h2, o_mla, o_moba, w_out):
    T, D = h2.shape
    tm = 512
    half = o_mla.shape[1]
    return pl.pallas_call(
        _attn_out_kernel,
        grid=(T // tm,),
        in_specs=[pl.BlockSpec((tm, D), lambda i: (i, 0)),
                  pl.BlockSpec((tm, half), lambda i: (i, 0)),
                  pl.BlockSpec((tm, half), lambda i: (i, 0)),
                  pl.BlockSpec(w_out.shape, lambda i: (0, 0))],
        out_specs=pl.BlockSpec((tm, D), lambda i: (i, 0)),
        out_shape=jax.ShapeDtypeStruct((T, D), F32),
        compiler_params=_params("parallel"),
        name="attn_out",
    )(h2, o_mla, o_moba, w_out)


def _attn_weights(w_in, q_norm, w_uq, kv_norm, w_ukv, w_out):
    D = w_in.shape[0]
    w_cq, w_ckv = w_in[:, 0:384], w_in[:, 384:640]
    w_kr = w_in[:, 640:672]
    w_mq, w_mk, w_mv = w_in[:, 672:1184], w_in[:, 1184:1696], w_in[:, 1696:2208]
    z = lambda n: jnp.zeros((D, n), F32)
    kr1 = jnp.concatenate([z(64), w_kr, z(32)], axis=1)
    kr2 = jnp.concatenate([z(64), -w_kr[:, 16:32], w_kr[:, 0:16], z(32)], axis=1)
    w1 = jnp.concatenate([w_cq, w_ckv, kr1, kr2, w_mv], axis=1).astype(BF16)

    w2 = jnp.concatenate([w_mq, w_mk], axis=1)
    w2h = w2.astype(BF16)
    w2l = (w2 - w2h.astype(F32)).astype(BF16)

    uq = w_uq.reshape(MLA_Q_LORA, MLA_HEADS, MLA_NOPE + MLA_ROPE)
    nope, pe1, pe2 = uq[..., :64], uq[..., 64:80], uq[..., 80:96]
    zq = lambda n: jnp.zeros((MLA_Q_LORA, MLA_HEADS, n), F32)
    wq = jnp.concatenate([nope, pe1, pe2, zq(32)], axis=-1).reshape(MLA_Q_LORA, 1024).astype(BF16)
    wqr = jnp.concatenate([zq(64), -pe2, pe1, zq(32)], axis=-1).reshape(MLA_Q_LORA, 1024).astype(BF16)

    ukv = w_ukv.reshape(MLA_KV_LORA, MLA_HEADS, MLA_NOPE + MLA_V)
    k_nope, v = ukv[..., :64], ukv[..., 64:]
    zk = jnp.zeros((MLA_KV_LORA, MLA_HEADS, 64), F32)
    wk = jnp.concatenate([k_nope, zk], axis=-1).reshape(MLA_KV_LORA, 1024).astype(BF16)
    even = (jnp.arange(MLA_HEADS) % 2 == 0)[None, :, None]
    wv = jnp.concatenate([jnp.where(even, v, 0.0), jnp.where(even, 0.0, v)], axis=-1)
    wv = wv.reshape(MLA_KV_LORA, 1024).astype(BF16)
    return dict(w1=w1, w2h=w2h, w2l=w2l, gq=q_norm[None, :], gkv=kv_norm[None, :], wq=wq, wqr=wqr,
                wk=wk, wv=wv, w_out=w_out.astype(BF16))


def _attention_layer(h, g, w, tabs):
    B, S, D = h.shape
    qm, km, vm, qb, qf, kf, kb, vb = _attn_prep(h, g[None, :], w, tabs)
    o_mla = _flash(qm, km, vm)
    sel = _moba_gate(qf, kf)
    o_moba = _flash(qb, kb, vb, sel)
    out = _attn_out(h.reshape(B * S, D), o_mla.reshape(B * S, -1), o_moba.reshape(B * S, -1), w["w_out"])
    return out.reshape(B, S, D)


def _conv_in_kernel(h_ref, g_ref, wu_ref, wg_ref, bu_ref, bg_ref, o_ref):
    xn = _rms(h_ref[...], g_ref[...]).astype(BF16)
    u = _dot(xn, wu_ref[...]) + bu_ref[...]
    gate = _dot(xn, wg_ref[...]) + bg_ref[...]
    o_ref[...] = u * jax.nn.sigmoid(gate)


def _conv_in(h2, g, w_in, b_in):
    T, D = h2.shape
    tm = 512
    const = lambda a: pl.BlockSpec(a.shape, lambda i: (0, 0))
    args = [g[None, :], w_in[:, :D].astype(BF16), w_in[:, D:].astype(BF16), b_in[None, :D], b_in[None, D:]]
    return pl.pallas_call(
        _conv_in_kernel,
        grid=(T // tm,),
        in_specs=[pl.BlockSpec((tm, D), lambda i: (i, 0))] + [const(a) for a in args],
        out_specs=pl.BlockSpec((tm, D), lambda i: (i, 0)),
        out_shape=jax.ShapeDtypeStruct((T, D), F32),
        compiler_params=_params("parallel"),
        name="conv_in",
    )(h2, *args)


def _conv_main_kernel(u_ref, up_ref, h_ref, wdw_ref, bdw_ref, lg_ref, lb_ref, wo_ref, bo_ref, o_ref,
                      buf_scr, y_scr):
    i = pl.program_id(1)
    ts = u_ref.shape[1]
    rows = CONV_HALO + ts
    buf_scr[0, 0:CONV_HALO, :] = jnp.where(i == 0, 0.0, up_ref[0])
    buf_scr[0, CONV_HALO:rows, :] = u_ref[0]
    for ph in range(1, SUBLANES):
        buf_scr[ph] = pltpu.roll(buf_scr[0], rows - ph, axis=0)
    lead = CONV_HALO - (CONV_WIDTH - 1)
    for r in range(ts // CONV_ROWS):
        acc = jnp.broadcast_to(bdw_ref[...], (CONV_ROWS, bdw_ref.shape[1]))
        for w in range(CONV_WIDTH):
            ph = (lead + w) % SUBLANES
            base = r * CONV_ROWS + (lead + w) - ph
            acc = acc + buf_scr[ph, base:base + CONV_ROWS, :] * jnp.tile(wdw_ref[w], (CONV_ROWS // SUBLANES, 1))
        mu = jnp.mean(acc, axis=-1, keepdims=True)
        cen = acc - mu
        var = jnp.mean(cen * cen, axis=-1, keepdims=True)
        y = cen * lax.rsqrt(var + NORM_EPS) * lg_ref[...] + lb_ref[...]
        y_scr[r * CONV_ROWS:(r + 1) * CONV_ROWS, :] = (y * jax.nn.sigmoid(y)).astype(BF16)
    o_ref[0] = h_ref[0] + (_dot(y_scr[...], wo_ref[...]) + bo_ref[...])


def _conv_main(u, h, w_dw, b_dw, ln_g, ln_b, w_out, b_out):
    B, S, D = h.shape
    ts = CONV_TS
    per = ts // CONV_HALO
    wdw = jnp.broadcast_to(w_dw[:, None, :], (CONV_WIDTH, SUBLANES, D))
    const = lambda a: pl.BlockSpec(a.shape, lambda b, i: (0,) * a.ndim)
    args = [wdw, b_dw[None, :], ln_g[None, :], ln_b[None, :], w_out.astype(BF16), b_out[None, :]]
    tile = pl.BlockSpec((1, ts, D), lambda b, i: (b, i, 0))
    return pl.pallas_call(
        _conv_main_kernel,
        grid=(B, S // ts),
        in_specs=[tile,
                  pl.BlockSpec((1, CONV_HALO, D), lambda b, i: (b, jnp.maximum(i * per - 1, 0), 0)),
                  tile] + [const(a) for a in args],
        out_specs=tile,
        out_shape=jax.ShapeDtypeStruct((B, S, D), F32),
        scratch_shapes=[pltpu.VMEM((SUBLANES, CONV_HALO + ts, D), F32), pltpu.VMEM((ts, D), BF16)],
        compiler_params=_params("parallel", "arbitrary"),
        name="conv_main",
    )(u, u, h, *args)


def _conv_layer(h, g, w_in, b_in, w_dw, b_dw, ln_g, ln_b, w_out, b_out):
    B, S, D = h.shape
    u = _conv_in(h.reshape(B * S, D), g, w_in, b_in).reshape(B, S, D)
    return _conv_main(u, h, w_dw, b_dw, ln_g, ln_b, w_out, b_out)


def _sort16_network():
    def merge(lo, hi, r):
        step = r * 2
        if step < hi - lo:
            yield from merge(lo, hi, step)
            yield from merge(lo + r, hi, step)
            yield from [(i, i + r) for i in range(lo + r, hi - r, step)]
        else:
            yield (lo, lo + r)

    def sort(lo, hi):
        if hi - lo >= 1:
            mid = lo + (hi - lo) // 2
            yield from sort(lo, mid)
            yield from sort(mid + 1, hi)
            yield from merge(lo, hi, 1)

    return tuple(sort(0, PEER_TOPK - 1))


def _top16_sorted(sc):
    g = [sc[SUBLANES * i:SUBLANES * (i + 1), :] for i in range(PEER_TOPK)]
    for i, j in _sort16_network():
        g[i], g[j] = jnp.maximum(g[i], g[j]), jnp.minimum(g[i], g[j])
    for shift in (4, 2, 1):
        y = [pltpu.roll(x, shift, axis=0) for x in g]
        z = [jnp.maximum(g[i], y[PEER_TOPK - 1 - i]) for i in range(PEER_TOPK)]
        for d in (8, 4, 2, 1):
            for i in range(PEER_TOPK):
                if i & d == 0:
                    z[i], z[i + d] = jnp.maximum(z[i], z[i + d]), jnp.minimum(z[i], z[i + d])
        g = z
    return g


def _peer_route_kernel(h_ref, g_ref, wqh_ref, wql_ref, skh_ref, skl_ref,
                       xn_ref, cnt_ref, e1_ref, rank_ref, e2_ref, q_scr, top_scr):
    xn = _rms(h_ref[...], g_ref[...])
    xh, xl = _split(xn)
    xn_ref[...] = xh
    q = _dot3(xh, xl, wqh_ref[...], wql_ref[...])
    for hc in range(2 * PEER_HEADS):
        q_scr[hc] = q[:, hc * PEER_D_HALF:(hc + 1) * PEER_D_HALF]

    def head(h, carry):
        s = []
        rank = None
        for c in range(2):
            qh, ql = _split(q_scr[2 * h + c])
            sc = _dot3_nt(skh_ref[2 * h + c], skl_ref[2 * h + c], qh, ql)
            s.append(sc)
            tops = _top16_sorted(sc)
            for k in range(PEER_TOPK):
                top_scr[c, k:k + 1, :] = tops[k][0:1, :]
            if c == 1:
                rank = jnp.full(sc.shape, float(PEER_TOPK), F32)
                grouped = sc.reshape(PEER_N_KEYS // SUBLANES, SUBLANES, sc.shape[1])
                rank = rank.reshape(grouped.shape)
                for k in reversed(range(PEER_TOPK)):
                    rank = jnp.where(grouped >= tops[k][None], float(k), rank)
                rank = rank.reshape(sc.shape)
        a = top_scr[0]
        b = top_scr[1]
        cands = [a[0:1] + b, a[1:2] + b[0:8]]
        cands += [a[i:i + 1] + b[0:8] for i in range(2, 8)]
        cands += [a[8:16] + b[0:1]]
        m_top = a[0:1] + b[0:1]
        pad = [jnp.full(cands[1].shape, -jnp.inf, F32)] * (PEER_TOPK - len(cands) - 1)
        best = _top16_sorted(jnp.concatenate(cands + pad, axis=0))
        thr = best[PEER_TOPK - 1][0:1, :]
        z = jnp.zeros_like(m_top)
        for k in range(PEER_TOPK):
            z = z + jnp.exp(best[k][0:1, :] - m_top)
        cnt = jnp.zeros(s[0].shape, F32)
        for j in range(PEER_TOPK):
            cnt = cnt + jnp.where(s[0] + b[j:j + 1] >= thr, 1.0, 0.0)
        cnt_ref[h] = cnt.astype(BF16)
        e1_ref[h] = (jnp.exp(s[0] - a[0:1]) * (0.5 / z)).astype(BF16)
        rank_ref[h] = rank.astype(BF16)
        e2_ref[h] = jnp.exp(s[1] - b[0:1]).astype(BF16)
        return carry

    lax.fori_loop(0, PEER_HEADS, head, 0)


def _peer_route(h2, g, wqh, wql, skh, skl):
    T, D = h2.shape
    tm = PEER_ROUTE_TM
    const = lambda a: pl.BlockSpec(a.shape, lambda i: (0,) * a.ndim)
    stat = pl.BlockSpec((PEER_HEADS, PEER_N_KEYS, tm), lambda i: (0, 0, i))
    a_shape = jax.ShapeDtypeStruct((PEER_HEADS, PEER_N_KEYS, T), BF16)
    b_shape = jax.ShapeDtypeStruct((PEER_HEADS, PEER_N_KEYS, T), BF16)
    args = [g[None, :], wqh, wql, skh, skl]
    return pl.pallas_call(
        _peer_route_kernel,
        grid=(T // tm,),
        in_specs=[pl.BlockSpec((tm, D), lambda i: (i, 0))] + [const(a) for a in args],
        out_specs=[pl.BlockSpec((tm, D), lambda i: (i, 0)), stat, stat, stat, stat],
        out_shape=[jax.ShapeDtypeStruct((T, D), BF16), a_shape, a_shape, b_shape, b_shape],
        scratch_shapes=[pltpu.VMEM((2 * PEER_HEADS, tm, PEER_D_HALF), F32),
                        pltpu.VMEM((2, PEER_TOPK, tm), F32)],
        compiler_params=_params("parallel"),
        name="peer_route",
    )(h2, *args)


def _gelu_x2(x):
    return x + x * lax.erf(x * (2.0 ** -0.5))


def _bcast_row(ref, h, a):
    row = ref[h, a:a + 1, :]
    return jnp.broadcast_to(row, (PEER_N_KEYS, row.shape[1]))


def _peer_main_kernel(xn_ref, u_ref, vt_ref, cnt_ref, e1_ref, rank_ref, e2_ref, h_ref, o_ref, acc_scr, w_scr):
    k = pl.program_id(1)
    tm = w_scr.shape[1]
    pair = 2 * PEER_N_KEYS

    @pl.when(k == 0)
    def _():
        acc_scr[...] = jnp.zeros(acc_scr.shape, F32)

    for p in range(PEER_A_TILE // 2):
        prow = slice(p * pair, (p + 1) * pair)
        st = _dot_nt(u_ref[prow, :], xn_ref[...])
        for r in range(2):
            a = 2 * p + r
            gsum = None
            for h in range(PEER_HEADS):
                cnt = _bcast_row(cnt_ref, h, a)
                e1 = _bcast_row(e1_ref, h, a)
                picked = rank_ref[h] < cnt
                term = jnp.where(picked, e2_ref[h] * e1, jnp.zeros((), BF16))
                gsum = term if gsum is None else gsum + term
            act = _gelu_x2(st[r * PEER_N_KEYS:(r + 1) * PEER_N_KEYS, :].astype(BF16))
            w_scr[a * PEER_N_KEYS:(a + 1) * PEER_N_KEYS, :] = gsum * act
    acc_scr[...] += _dot(vt_ref[...], w_scr[...])

    @pl.when(k == pl.num_programs(1) - 1)
    def _():
        o_ref[...] = h_ref[...] + acc_scr[...].T


def _peer_main(h2, xn, u_bf, vt_bf, cnt, e1, rank, e2):
    T, D = h2.shape
    E = u_bf.shape[0]
    tm = PEER_TM
    te = PEER_A_TILE * PEER_N_KEYS
    tok = pl.BlockSpec((tm, D), lambda i, k: (i, 0))
    a_stat = pl.BlockSpec((PEER_HEADS, PEER_A_TILE, tm), lambda i, k: (0, k, i))
    b_stat = pl.BlockSpec((PEER_HEADS, PEER_N_KEYS, tm), lambda i, k: (0, 0, i))
    return pl.pallas_call(
        _peer_main_kernel,
        grid=(T // tm, E // te),
        in_specs=[tok,
                  pl.BlockSpec((te, D), lambda i, k: (k, 0)),
                  pl.BlockSpec((D, te), lambda i, k: (0, k)),
                  a_stat, a_stat, b_stat, b_stat, tok],
        out_specs=tok,
        out_shape=jax.ShapeDtypeStruct((T, D), F32),
        scratch_shapes=[pltpu.VMEM((D, tm), F32), pltpu.VMEM((te, tm), BF16)],
        compiler_params=_params("parallel", "arbitrary"),
        name="peer_main",
    )(xn, u_bf, vt_bf, cnt, e1, rank, e2, h2)


def _peer_layer(h, g, w_q, subkeys, u_tab, v_tab):
    B, S, D = h.shape
    h2 = h.reshape(B * S, D)
    wqh = w_q.astype(BF16)
    wql = (w_q - wqh.astype(F32)).astype(BF16)
    sk = subkeys.reshape(2 * PEER_HEADS, PEER_N_KEYS, PEER_D_HALF)
    skh = sk.astype(BF16)
    skl = (sk - skh.astype(F32)).astype(BF16)
    xn, cnt, e1, rank, e2 = _peer_route(h2, g, wqh, wql, skh, skl)
    out = _peer_main(h2, xn, u_tab.astype(BF16), v_tab.T.astype(BF16), cnt, e1, rank, e2)
    return out.reshape(B, S, D)


def _final_norm_kernel(h_ref, g_ref, o_ref):
    o_ref[...] = _rms(h_ref[...], g_ref[...])


def _final_norm(h2, g):
    T, D = h2.shape
    tm = 512
    return pl.pallas_call(
        _final_norm_kernel,
        grid=(T // tm,),
        in_specs=[pl.BlockSpec((tm, D), lambda i: (i, 0)), pl.BlockSpec((1, D), lambda i: (0, 0))],
        out_specs=pl.BlockSpec((tm, D), lambda i: (i, 0)),
        out_shape=jax.ShapeDtypeStruct((T, D), F32),
        compiler_params=_params("parallel"),
        name="final_norm",
    )(h2, g[None, :])


def kernel(x, positions, norm_mix, norm_ffn, norm_final, attn_w_in, mla_q_norm, mla_w_uq, mla_kv_norm,
           mla_w_ukv, attn_w_out, conv_w_in, conv_b_in, conv_w_dw, conv_b_dw, conv_ln_g, conv_ln_b,
           conv_w_out, conv_b_out, peer_w_q, peer_subkeys, peer_u, peer_v):
    B, S, D = x.shape
    assert D == D_MODEL and S % ATTN_TQ == 0 and S % MOBA_BLOCK == 0 and (B * S) % PEER_TM == 0
    depth = norm_mix.shape[0]
    tabs = _rope_tables(positions)
    h = x
    for layer in range(depth):
        i = layer // 2
        if layer % 2 == 0:
            w = _attn_weights(attn_w_in[i], mla_q_norm[i], mla_w_uq[i], mla_kv_norm[i], mla_w_ukv[i],
                              attn_w_out[i])
            h = _attention_layer(h, norm_mix[layer], w, tabs)
        else:
            h = _conv_layer(h, norm_mix[layer], conv_w_in[i], conv_b_in[i], conv_w_dw[i], conv_b_dw[i],
                            conv_ln_g[i], conv_ln_b[i], conv_w_out[i], conv_b_out[i])
        h = _peer_layer(h, norm_ffn[layer], peer_w_q[layer], peer_subkeys[layer], peer_u[layer],
                        peer_v[layer])
    return _final_norm(h.reshape(B * S, D), norm_final).reshape(B, S, D)
```

```python
import functools

import jax
import jax.numpy as jnp
from jax import lax
from jax.experimental import pallas as pl
from jax.experimental.pallas import tpu as pltpu

F32 = jnp.float32
BF16 = jnp.bfloat16

D_MODEL = 1024
ROPE_THETA = 10000.0
NORM_EPS = 1e-6

MLA_HEADS = 8
MLA_Q_LORA = 384
MLA_KV_LORA = 256
MLA_NOPE = 64
MLA_ROPE = 32
MLA_V = 64

MOBA_HEADS = 8
MOBA_HEAD_DIM = 64
MOBA_BLOCK = 256
MOBA_TOPK = 3

CONV_WIDTH = 31

PEER_HEADS = 8
PEER_N_KEYS = 128
PEER_D_HALF = 128
PEER_TOPK = 16

LANES = 128
SUBLANES = 8
HEAD_PAIRS = 4
VMEM_LIMIT = 56 * 1024 * 1024
NEG = -1e30
LOG2E = 1.4426950408889634

ATTN_TQ = 512
ATTN_TK = 512
FLASH_PAIRS = 2
PROJ_TM = 256
PEER_ROUTE_TM = 256
PEER_TM = 512
PEER_A_TILE = 32
CONV_TS = 512
CONV_HALO = 32
CONV_ROWS = 32


def _params(*sem):
    return pltpu.CompilerParams(dimension_semantics=sem, vmem_limit_bytes=VMEM_LIMIT)


def _dot(a, b):
    return jnp.dot(a, b, preferred_element_type=F32)


def _dot_nt(a, b):
    return lax.dot_general(a, b, (((1,), (1,)), ((), ())), preferred_element_type=F32)


def _split(a):
    hi = a.astype(BF16)
    lo = (a - hi.astype(F32)).astype(BF16)
    return hi, lo


def _dot3(ah, al, bh, bl):
    return _dot(ah, bh) + (_dot(ah, bl) + _dot(al, bh))


def _dot3_nt(ah, al, bh, bl):
    return _dot_nt(ah, bh) + (_dot_nt(ah, bl) + _dot_nt(al, bh))


def _rms(x, g):
    return x * lax.rsqrt(jnp.mean(x * x, axis=-1, keepdims=True) + NORM_EPS) * g


def _rope_kernel(pos_ref, fr_ref, fm_ref, cr_ref, sr_ref, cm_ref, sm_ref):
    pos = pos_ref[0].astype(F32)
    ang_r = pos * fr_ref[0:1, :]
    ang_m = pos * fm_ref[0:1, :]
    cr_ref[0] = jnp.cos(ang_r) * fr_ref[1:2, :] + fr_ref[2:3, :]
    sr_ref[0] = jnp.sin(ang_r) * fr_ref[1:2, :]
    cm_ref[0] = jnp.cos(ang_m)
    sm_ref[0] = jnp.sin(ang_m) * fm_ref[1:2, :]


def _rope_tables(positions):
    B, S = positions.shape
    ts = 512
    inv_r = 1.0 / (ROPE_THETA ** (jnp.arange(0, MLA_ROPE, 2, dtype=F32) / MLA_ROPE))
    inv_m = 1.0 / (ROPE_THETA ** (jnp.arange(0, MOBA_HEAD_DIM, 2, dtype=F32) / MOBA_HEAD_DIM))
    z32 = jnp.zeros((32,), F32)
    z64 = jnp.zeros((64,), F32)
    fr = jnp.stack([jnp.concatenate([z64, inv_r, inv_r, z32]),
                    jnp.concatenate([z64, jnp.ones((32,), F32), z32]),
                    jnp.concatenate([jnp.ones((64,), F32), z32, z32])])
    fr = jnp.concatenate([fr, jnp.zeros((5, LANES), F32)])
    sign = jnp.tile(jnp.concatenate([-jnp.ones((32,), F32), jnp.ones((32,), F32)]), 2)
    fm = jnp.concatenate([jnp.tile(inv_m, 4)[None, :], sign[None, :], jnp.zeros((6, LANES), F32)])
    tab = jax.ShapeDtypeStruct((B, S, LANES), F32)
    row = pl.BlockSpec((1, ts, LANES), lambda b, i: (b, i, 0))
    return pl.pallas_call(
        _rope_kernel,
        grid=(B, S // ts),
        in_specs=[pl.BlockSpec((1, ts, 1), lambda b, i: (b, i, 0)),
                  pl.BlockSpec((8, LANES), lambda b, i: (0, 0)),
                  pl.BlockSpec((8, LANES), lambda b, i: (0, 0))],
        out_specs=[row, row, row, row],
        out_shape=[tab, tab, tab, tab],
        compiler_params=_params("parallel", "parallel"),
        name="rope_tables",
    )(positions.reshape(B, S, 1), fr, fm)


def _attn_prep_kernel(h_ref, g_ref, w1_ref, w2h_ref, w2l_ref, gq_ref, gkv_ref, wq_ref, wqr_ref,
                      wk_ref, wv_ref, cr_ref, sr_ref, cm_ref, sm_ref,
                      qm_ref, km_ref, vm_ref, qb_ref, qf_ref, kf_ref, kb_ref, vb_ref):
    xn = _rms(h_ref[0], g_ref[...])
    xh, xl = _split(xn)
    p1 = _dot(xh, w1_ref[...])
    p2 = _dot3(xh, xl, w2h_ref[...], w2l_ref[...])
    cq, ckv = p1[:, 0:384], p1[:, 384:640]
    kr1, kr2 = p1[:, 640:768], p1[:, 768:896]
    mv = p1[:, 896:1408]
    cr, sr, cm, sm = cr_ref[0], sr_ref[0], cm_ref[0], sm_ref[0]
    nq = _rms(cq, gq_ref[...]).astype(BF16)
    nkv = _rms(ckv, gkv_ref[...]).astype(BF16)
    qa = _dot(nq, wq_ref[...])
    qr = _dot(nq, wqr_ref[...])
    ka = _dot(nkv, wk_ref[...])
    va = _dot(nkv, wv_ref[...])
    kpe = kr1 * cr + kr2 * sr
    mla_scale = (MLA_NOPE + MLA_ROPE) ** -0.5 * LOG2E
    for h in range(MLA_HEADS):
        sl = slice(h * LANES, (h + 1) * LANES)
        qm_ref[0, h] = ((qa[:, sl] * cr + qr[:, sl] * sr) * mla_scale).astype(BF16)
        km_ref[0, h] = (ka[:, sl] + kpe).astype(BF16)
        vm_ref[0, h] = va[:, sl].astype(BF16)
    lane = lax.broadcasted_iota(jnp.int32, cm.shape, 1)
    low = lane < MOBA_HEAD_DIM
    first = (lane & (MOBA_HEAD_DIM - 1)) < MOBA_HEAD_DIM // 2

    def rope(x):
        half = MOBA_HEAD_DIM // 2
        return x * cm + jnp.where(first, pltpu.roll(x, LANES - half, axis=1), pltpu.roll(x, half, axis=1)) * sm

    moba_scale = MOBA_HEAD_DIM ** -0.5 * LOG2E
    for p in range(HEAD_PAIRS):
        sl = slice(p * LANES, (p + 1) * LANES)
        q = rope(p2[:, sl])
        k = rope(p2[:, 512 + p * LANES:512 + (p + 1) * LANES])
        v = mv[:, sl]
        qf_ref[0, :, sl] = q
        kf_ref[0, :, sl] = k
        qb_ref[0, :, sl] = (q * moba_scale).astype(BF16)
        kb_ref[0, 2 * p] = jnp.where(low, k, 0.0).astype(BF16)
        kb_ref[0, 2 * p + 1] = jnp.where(low, 0.0, k).astype(BF16)
        vb_ref[0, 2 * p] = jnp.where(low, v, 0.0).astype(BF16)
        vb_ref[0, 2 * p + 1] = jnp.where(low, 0.0, v).astype(BF16)


def _attn_prep(h, g, w, tabs):
    B, S, D = h.shape
    tm = PROJ_TM
    const = lambda a: pl.BlockSpec(a.shape, lambda b, i: (0,) * a.ndim)
    row = lambda n: pl.BlockSpec((1, tm, n), lambda b, i: (b, i, 0))
    head = pl.BlockSpec((1, 8, tm, LANES), lambda b, i: (b, 0, i, 0))
    head_shape = jax.ShapeDtypeStruct((B, 8, S, LANES), BF16)
    weights = [g, w["w1"], w["w2h"], w["w2l"], w["gq"], w["gkv"], w["wq"], w["wqr"], w["wk"], w["wv"]]
    return pl.pallas_call(
        _attn_prep_kernel,
        grid=(B, S // tm),
        in_specs=[row(D)] + [const(a) for a in weights] + [row(LANES)] * 4,
        out_specs=[head, head, head, row(512), row(512), row(512), head, head],
        out_shape=[head_shape, head_shape, head_shape,
                   jax.ShapeDtypeStruct((B, S, 512), BF16),
                   jax.ShapeDtypeStruct((B, S, 512), F32),
                   jax.ShapeDtypeStruct((B, S, 512), F32),
                   head_shape, head_shape],
        compiler_params=_params("parallel", "parallel"),
        name="attn_prep",
    )(h, *weights, *tabs)


def _moba_gate_kernel(q_ref, k_ref, sel_ref, kmh_ref, kml_ref):
    i = pl.program_id(2)
    tq = q_ref.shape[1]
    S = k_ref.shape[1]
    nb = S // MOBA_BLOCK

    @pl.when(i == 0)
    def _():
        km = jnp.sum(k_ref[0].reshape(nb, MOBA_BLOCK, LANES), axis=1) / float(MOBA_BLOCK)
        low = lax.broadcasted_iota(jnp.int32, km.shape, 1) < MOBA_HEAD_DIM
        for hh, m in enumerate((jnp.where(low, km, 0.0), jnp.where(low, 0.0, km))):
            hi, lo = _split(m)
            kmh_ref[hh] = hi
            kml_ref[hh] = lo

    qh, ql = _split(q_ref[0])
    row = i * tq + lax.broadcasted_iota(jnp.int32, (tq, nb), 0)
    blk = lax.broadcasted_iota(jnp.int32, (tq, nb), 1)
    valid = blk * MOBA_BLOCK < row - (row & (MOBA_BLOCK - 1))
    for hh in range(2):
        gate = _dot3_nt(qh, ql, kmh_ref[hh], kml_ref[hh])
        g = jnp.where(valid, gate, -jnp.inf)
        rest = g
        for _ in range(MOBA_TOPK - 1):
            top = jnp.max(rest, axis=1, keepdims=True)
            rest = jnp.where(rest == top, -jnp.inf, rest)
        thr = jnp.max(rest, axis=1, keepdims=True)
        sel_ref[0, hh] = jnp.where(valid & (g >= thr), 1.0, 0.0)


def _moba_gate(qf, kf):
    B, S, _ = qf.shape
    tq = ATTN_TQ
    nb = S // MOBA_BLOCK
    return pl.pallas_call(
        _moba_gate_kernel,
        grid=(B, HEAD_PAIRS, S // tq),
        in_specs=[pl.BlockSpec((1, tq, LANES), lambda b, p, i: (b, i, p)),
                  pl.BlockSpec((1, S, LANES), lambda b, p, i: (b, 0, p))],
        out_specs=pl.BlockSpec((1, 2, tq, nb), lambda b, p, i: (b, p, i, 0)),
        out_shape=jax.ShapeDtypeStruct((B, 8, S, nb), F32),
        scratch_shapes=[pltpu.VMEM((2, nb, LANES), BF16), pltpu.VMEM((2, nb, LANES), BF16)],
        compiler_params=_params("parallel", "parallel", "arbitrary"),
        name="moba_gate",
    )(qf, kf)


def _flash_kernel(*refs, moba, paired_q):
    if moba:
        q_ref, k_ref, v_ref, sel_ref, o_ref, m_scr, l_scr, acc_scr = refs
    else:
        q_ref, k_ref, v_ref, o_ref, m_scr, l_scr, acc_scr = refs
    i = pl.program_id(2)
    tq, tk = ATTN_TQ, ATTN_TK
    heads = 2 * FLASH_PAIRS
    if paired_q:
        qs = [q_ref[0, :, (hd // 2) * LANES:(hd // 2 + 1) * LANES] for hd in range(heads)]
    else:
        qs = [q_ref[0, hd] for hd in range(heads)]
    m_scr[...] = jnp.full(m_scr.shape, NEG, F32)
    l_scr[...] = jnp.zeros(l_scr.shape, F32)
    acc_scr[...] = jnp.zeros(acc_scr.shape, F32)
    low = lax.broadcasted_iota(jnp.int32, (tq, LANES), 1) < 64

    def step(j, diag):
        start = pl.multiple_of(j * tk, tk)
        for pp in range(FLASH_PAIRS):
            alphas, pvs = [], []
            for hh in range(2):
                hd = 2 * pp + hh
                k = k_ref[0, hd, pl.ds(start, tk), :]
                v = v_ref[0, hd, pl.ds(start, tk), :]
                s = _dot_nt(qs[hd], k)
                if moba:
                    sel = sel_ref[0, hd]
                    col = lax.broadcasted_iota(jnp.int32, sel.shape, 1)
                    sub = tk // MOBA_BLOCK
                    picked = jnp.concatenate(
                        [jnp.broadcast_to(
                            jnp.sum(jnp.where(col == j * sub + r, sel, 0.0), axis=1, keepdims=True),
                            (tq, MOBA_BLOCK)) for r in range(sub)], axis=1) > 0.5
                if diag:
                    qpos = i * tq + lax.broadcasted_iota(jnp.int32, (tq, tk), 0)
                    kpos = j * tk + lax.broadcasted_iota(jnp.int32, (tq, tk), 1)
                    ok = kpos <= qpos
                    if moba:
                        ok = (ok & (kpos >= qpos - (qpos & (MOBA_BLOCK - 1)))) | picked
                    s = jnp.where(ok, s, NEG)
                elif moba:
                    s = jnp.where(picked, s, NEG)
                m_prev = m_scr[hd]
                m_next = jnp.maximum(m_prev, jnp.max(s, axis=1, keepdims=True))
                alpha = jnp.exp2(m_prev - m_next)
                p = jnp.exp2(s - jnp.tile(m_next, (1, tk // LANES)))
                l_scr[hd] = alpha * l_scr[hd] + jnp.sum(p, axis=1, keepdims=True)
                m_scr[hd] = m_next
                alphas.append(alpha)
                pvs.append(_dot(p.astype(BF16), v))
            acc_scr[pp] = acc_scr[pp] * jnp.where(low, alphas[0], alphas[1]) + (pvs[0] + pvs[1])

    n_full = i * (tq // tk)

    def body(j, carry):
        step(j, False)
        return carry

    lax.fori_loop(0, n_full, body, 0)
    for r in range(tq // tk):
        step(n_full + r, True)
    for pp in range(FLASH_PAIRS):
        o_ref[0, :, pp * LANES:(pp + 1) * LANES] = (
            acc_scr[pp] / jnp.where(low, l_scr[2 * pp], l_scr[2 * pp + 1])).astype(o_ref.dtype)


def _flash(q, k, v, sel=None):
    moba = sel is not None
    B, _, S, _ = k.shape
    tq = ATTN_TQ
    np_, heads = FLASH_PAIRS, 2 * FLASH_PAIRS
    kv_spec = pl.BlockSpec((1, heads, S, LANES), lambda b, p, i: (b, p, 0, 0))
    if moba:
        q_spec = pl.BlockSpec((1, tq, np_ * LANES), lambda b, p, i: (b, i, p))
        extra = [pl.BlockSpec((1, heads, tq, sel.shape[3]), lambda b, p, i: (b, p, i, 0))]
        args = (q, k, v, sel)
    else:
        q_spec = pl.BlockSpec((1, heads, tq, LANES), lambda b, p, i: (b, p, i, 0))
        extra = []
        args = (q, k, v)
    return pl.pallas_call(
        functools.partial(_flash_kernel, moba=moba, paired_q=moba),
        grid=(B, HEAD_PAIRS // np_, S // tq),
        in_specs=[q_spec, kv_spec, kv_spec] + extra,
        out_specs=pl.BlockSpec((1, tq, np_ * LANES), lambda b, p, i: (b, i, p)),
        out_shape=jax.ShapeDtypeStruct((B, S, HEAD_PAIRS * LANES), BF16),
        scratch_shapes=[pltpu.VMEM((heads, tq, LANES), F32), pltpu.VMEM((heads, tq, LANES), F32),
                        pltpu.VMEM((np_, tq, LANES), F32)],
        compiler_params=_params("parallel", "parallel", "arbitrary"),
        name="flash_moba" if moba else "flash_mla",
    )(*args)


def _attn_out_kernel(h_ref, oa_ref, ob_ref, w_ref, o_ref):
    half = oa_ref.shape[1]
    o_ref[...] = h_ref[...] + (_dot(oa_ref[...], w_ref[0:half, :]) + _dot(ob_ref[...], w_ref[half:, :]))


def _attn_out(h2, o_mla, o_moba, w_out):
    T, D = h2.shape
    tm = 512
    half = o_mla.shape[1]
    return pl.pallas_call(
        _attn_out_kernel,
        grid=(T // tm,),
        in_specs=[pl.BlockSpec((tm, D), lambda i: (i, 0)),
                  pl.BlockSpec((tm, half), lambda i: (i, 0)),
                  pl.BlockSpec((tm, half), lambda i: (i, 0)),
                  pl.BlockSpec(w_out.shape, lambda i: (0, 0))],
        out_specs=pl.BlockSpec((tm, D), lambda i: (i, 0)),
        out_shape=jax.ShapeDtypeStruct((T, D), F32),
        compiler_params=_params("parallel"),
        name="attn_out",
    )(h2, o_mla, o_moba, w_out)


def _attn_weights(w_in, q_norm, w_uq, kv_norm, w_ukv, w_out):
    D = w_in.shape[0]
    w_cq, w_ckv = w_in[:, 0:384], w_in[:, 384:640]
    w_kr = w_in[:, 640:672]
    w_mq, w_mk, w_mv = w_in[:, 672:1184], w_in[:, 1184:1696], w_in[:, 1696:2208]
    z = lambda n: jnp.zeros((D, n), F32)
    kr1 = jnp.concatenate([z(64), w_kr, z(32)], axis=1)
    kr2 = jnp.concatenate([z(64), -w_kr[:, 16:32], w_kr[:, 0:16], z(32)], axis=1)
    w1 = jnp.concatenate([w_cq, w_ckv, kr1, kr2, w_mv], axis=1).astype(BF16)

    w2 = jnp.concatenate([w_mq, w_mk], axis=1)
    w2h = w2.astype(BF16)
    w2l = (w2 - w2h.astype(F32)).astype(BF16)

    uq = w_uq.reshape(MLA_Q_LORA, MLA_HEADS, MLA_NOPE + MLA_ROPE)
    nope, pe1, pe2 = uq[..., :64], uq[..., 64:80], uq[..., 80:96]
    zq = lambda n: jnp.zeros((MLA_Q_LORA, MLA_HEADS, n), F32)
    wq = jnp.concatenate([nope, pe1, pe2, zq(32)], axis=-1).reshape(MLA_Q_LORA, 1024).astype(BF16)
    wqr = jnp.concatenate([zq(64), -pe2, pe1, zq(32)], axis=-1).reshape(MLA_Q_LORA, 1024).astype(BF16)

    ukv = w_ukv.reshape(MLA_KV_LORA, MLA_HEADS, MLA_NOPE + MLA_V)
    k_nope, v = ukv[..., :64], ukv[..., 64:]
    zk = jnp.zeros((MLA_KV_LORA, MLA_HEADS, 64), F32)
    wk = jnp.concatenate([k_nope, zk], axis=-1).reshape(MLA_KV_LORA, 1024).astype(BF16)
    even = (jnp.arange(MLA_HEADS) % 2 == 0)[None, :, None]
    wv = jnp.concatenate([jnp.where(even, v, 0.0), jnp.where(even, 0.0, v)], axis=-1)
    wv = wv.reshape(MLA_KV_LORA, 1024).astype(BF16)
    return dict(w1=w1, w2h=w2h, w2l=w2l, gq=q_norm[None, :], gkv=kv_norm[None, :], wq=wq, wqr=wqr,
                wk=wk, wv=wv, w_out=w_out.astype(BF16))


def _attention_layer(h, g, w, tabs):
    B, S, D = h.shape
    qm, km, vm, qb, qf, kf, kb, vb = _attn_prep(h, g[None, :], w, tabs)
    o_mla = _flash(qm, km, vm)
    sel = _moba_gate(qf, kf)
    o_moba = _flash(qb, kb, vb, sel)
    out = _attn_out(h.reshape(B * S, D), o_mla.reshape(B * S, -1), o_moba.reshape(B * S, -1), w["w_out"])
    return out.reshape(B, S, D)


def _conv_in_kernel(h_ref, g_ref, wu_ref, wg_ref, bu_ref, bg_ref, o_ref):
    xn = _rms(h_ref[...], g_ref[...]).astype(BF16)
    u = _dot(xn, wu_ref[...]) + bu_ref[...]
    gate = _dot(xn, wg_ref[...]) + bg_ref[...]
    o_ref[...] = u * jax.nn.sigmoid(gate)


def _conv_in(h2, g, w_in, b_in):
    T, D = h2.shape
    tm = 512
    const = lambda a: pl.BlockSpec(a.shape, lambda i: (0, 0))
    args = [g[None, :], w_in[:, :D].astype(BF16), w_in[:, D:].astype(BF16), b_in[None, :D], b_in[None, D:]]
    return pl.pallas_call(
        _conv_in_kernel,
        grid=(T // tm,),
        in_specs=[pl.BlockSpec((tm, D), lambda i: (i, 0))] + [const(a) for a in args],
        out_specs=pl.BlockSpec((tm, D), lambda i: (i, 0)),
        out_shape=jax.ShapeDtypeStruct((T, D), F32),
        compiler_params=_params("parallel"),
        name="conv_in",
    )(h2, *args)


def _conv_main_kernel(u_ref, up_ref, h_ref, wdw_ref, bdw_ref, lg_ref, lb_ref, wo_ref, bo_ref, o_ref,
                      buf_scr, y_scr):
    i = pl.program_id(1)
    ts = u_ref.shape[1]
    rows = CONV_HALO + ts
    buf_scr[0, 0:CONV_HALO, :] = jnp.where(i == 0, 0.0, up_ref[0])
    buf_scr[0, CONV_HALO:rows, :] = u_ref[0]
    for ph in range(1, SUBLANES):
        buf_scr[ph] = pltpu.roll(buf_scr[0], rows - ph, axis=0)
    lead = CONV_HALO - (CONV_WIDTH - 1)
    for r in range(ts // CONV_ROWS):
        halves = []
        width = bdw_ref.shape[1] // 2
        for hf in range(2):
            cs = slice(hf * width, (hf + 1) * width)
            part = jnp.broadcast_to(bdw_ref[:, cs], (CONV_ROWS, width))
            for w in range(CONV_WIDTH):
                ph = (lead + w) % SUBLANES
                base = r * CONV_ROWS + (lead + w) - ph
                part = part + buf_scr[ph, base:base + CONV_ROWS, cs] * jnp.tile(wdw_ref[w, :, cs],
                                                                             (CONV_ROWS // SUBLANES, 1))
            halves.append(part)
        acc = jnp.concatenate(halves, axis=1)
        mu = jnp.mean(acc, axis=-1, keepdims=True)
        cen = acc - mu
        var = jnp.mean(cen * cen, axis=-1, keepdims=True)
        y = cen * lax.rsqrt(var + NORM_EPS) * lg_ref[...] + lb_ref[...]
        y_scr[r * CONV_ROWS:(r + 1) * CONV_ROWS, :] = (y * jax.nn.sigmoid(y)).astype(BF16)
    o_ref[0] = h_ref[0] + (_dot(y_scr[...], wo_ref[...]) + bo_ref[...])


def _conv_main(u, h, w_dw, b_dw, ln_g, ln_b, w_out, b_out):
    B, S, D = h.shape
    ts = CONV_TS
    per = ts // CONV_HALO
    wdw = jnp.broadcast_to(w_dw[:, None, :], (CONV_WIDTH, SUBLANES, D))
    const = lambda a: pl.BlockSpec(a.shape, lambda b, i: (0,) * a.ndim)
    args = [wdw, b_dw[None, :], ln_g[None, :], ln_b[None, :], w_out.astype(BF16), b_out[None, :]]
    tile = pl.BlockSpec((1, ts, D), lambda b, i: (b, i, 0))
    return pl.pallas_call(
        _conv_main_kernel,
        grid=(B, S // ts),
        in_specs=[tile,
                  pl.BlockSpec((1, CONV_HALO, D), lambda b, i: (b, jnp.maximum(i * per - 1, 0), 0)),
                  tile] + [const(a) for a in args],
        out_specs=tile,
        out_shape=jax.ShapeDtypeStruct((B, S, D), F32),
        scratch_shapes=[pltpu.VMEM((SUBLANES, CONV_HALO + ts, D), F32), pltpu.VMEM((ts, D), BF16)],
        compiler_params=_params("parallel", "arbitrary"),
        name="conv_main",
    )(u, u, h, *args)


def _conv_layer(h, g, w_in, b_in, w_dw, b_dw, ln_g, ln_b, w_out, b_out):
    B, S, D = h.shape
    u = _conv_in(h.reshape(B * S, D), g, w_in, b_in).reshape(B, S, D)
    return _conv_main(u, h, w_dw, b_dw, ln_g, ln_b, w_out, b_out)


def _sort16_network():
    def merge(lo, hi, r):
        step = r * 2
        if step < hi - lo:
            yield from merge(lo, hi, step)
            yield from merge(lo + r, hi, step)
            yield from [(i, i + r) for i in range(lo + r, hi - r, step)]
        else:
            yield (lo, lo + r)

    def sort(lo, hi):
        if hi - lo >= 1:
            mid = lo + (hi - lo) // 2
            yield from sort(lo, mid)
            yield from sort(mid + 1, hi)
            yield from merge(lo, hi, 1)

    return tuple(sort(0, PEER_TOPK - 1))


def _top16_sorted(sc):
    g = [sc[SUBLANES * i:SUBLANES * (i + 1), :] for i in range(PEER_TOPK)]
    for i, j in _sort16_network():
        g[i], g[j] = jnp.maximum(g[i], g[j]), jnp.minimum(g[i], g[j])
    for shift in (4, 2, 1):
        y = [pltpu.roll(x, shift, axis=0) for x in g]
        z = [jnp.maximum(g[i], y[PEER_TOPK - 1 - i]) for i in range(PEER_TOPK)]
        for d in (8, 4, 2, 1):
            for i in range(PEER_TOPK):
                if i & d == 0:
                    z[i], z[i + d] = jnp.maximum(z[i], z[i + d]), jnp.minimum(z[i], z[i + d])
        g = z
    return g


def _peer_route_kernel(h_ref, g_ref, wqh_ref, wql_ref, skh_ref, skl_ref,
                       xn_ref, cnt_ref, e1_ref, rank_ref, e2_ref, q_scr, top_scr):
    xn = _rms(h_ref[...], g_ref[...])
    xh, xl = _split(xn)
    xn_ref[...] = xh
    q = _dot3(xh, xl, wqh_ref[...], wql_ref[...])
    for hc in range(2 * PEER_HEADS):
        q_scr[hc] = q[:, hc * PEER_D_HALF:(hc + 1) * PEER_D_HALF]

    def head(h, carry):
        s = []
        rank = None
        for c in range(2):
            qh, ql = _split(q_scr[2 * h + c])
            sc = _dot3_nt(skh_ref[2 * h + c], skl_ref[2 * h + c], qh, ql)
            s.append(sc)
            tops = _top16_sorted(sc)
            for k in range(PEER_TOPK):
                top_scr[c, k:k + 1, :] = tops[k][0:1, :]
            if c == 1:
                rank = jnp.full(sc.shape, float(PEER_TOPK), F32)
                grouped = sc.reshape(PEER_N_KEYS // SUBLANES, SUBLANES, sc.shape[1])
                rank = rank.reshape(grouped.shape)
                for k in reversed(range(PEER_TOPK)):
                    rank = jnp.where(grouped >= tops[k][None], float(k), rank)
                rank = rank.reshape(sc.shape)
        a = top_scr[0]
        b = top_scr[1]
        cands = [a[0:1] + b, a[1:2] + b[0:8]]
        cands += [a[i:i + 1] + b[0:8] for i in range(2, 8)]
        cands += [a[8:16] + b[0:1]]
        m_top = a[0:1] + b[0:1]
        pad = [jnp.full(cands[1].shape, -jnp.inf, F32)] * (PEER_TOPK - len(cands) - 1)
        best = _top16_sorted(jnp.concatenate(cands + pad, axis=0))
        thr = best[PEER_TOPK - 1][0:1, :]
        z = jnp.zeros_like(m_top)
        for k in range(PEER_TOPK):
            z = z + jnp.exp(best[k][0:1, :] - m_top)
        cnt = jnp.zeros(s[0].shape, F32)
        for j in range(PEER_TOPK):
            cnt = cnt + jnp.where(s[0] + b[j:j + 1] >= thr, 1.0, 0.0)
        cnt_ref[h] = cnt.astype(BF16)
        e1_ref[h] = (jnp.exp(s[0] - a[0:1]) * (0.5 / z)).astype(BF16)
        rank_ref[h] = rank.astype(BF16)
        e2_ref[h] = jnp.exp(s[1] - b[0:1]).astype(BF16)
        return carry

    lax.fori_loop(0, PEER_HEADS, head, 0)


def _peer_route(h2, g, wqh, wql, skh, skl):
    T, D = h2.shape
    tm = PEER_ROUTE_TM
    const = lambda a: pl.BlockSpec(a.shape, lambda i: (0,) * a.ndim)
    stat = pl.BlockSpec((PEER_HEADS, PEER_N_KEYS, tm), lambda i: (0, 0, i))
    a_shape = jax.ShapeDtypeStruct((PEER_HEADS, PEER_N_KEYS, T), BF16)
    b_shape = jax.ShapeDtypeStruct((PEER_HEADS, PEER_N_KEYS, T), BF16)
    args = [g[None, :], wqh, wql, skh, skl]
    return pl.pallas_call(
        _peer_route_kernel,
        grid=(T // tm,),
        in_specs=[pl.BlockSpec((tm, D), lambda i: (i, 0))] + [const(a) for a in args],
        out_specs=[pl.BlockSpec((tm, D), lambda i: (i, 0)), stat, stat, stat, stat],
        out_shape=[jax.ShapeDtypeStruct((T, D), BF16), a_shape, a_shape, b_shape, b_shape],
        scratch_shapes=[pltpu.VMEM((2 * PEER_HEADS, tm, PEER_D_HALF), F32),
                        pltpu.VMEM((2, PEER_TOPK, tm), F32)],
        compiler_params=_params("parallel"),
        name="peer_route",
    )(h2, *args)


def _gelu_x2(x):
    return x + x * lax.erf(x * (2.0 ** -0.5))


def _bcast_row(ref, h, a):
    row = ref[h, a:a + 1, :]
    return jnp.broadcast_to(row, (PEER_N_KEYS, row.shape[1]))


def _peer_main_kernel(xn_ref, u_ref, vt_ref, cnt_ref, e1_ref, rank_ref, e2_ref, h_ref, o_ref, acc_scr, w_scr):
    k = pl.program_id(1)
    tm = w_scr.shape[1]
    pair = 2 * PEER_N_KEYS

    @pl.when(k == 0)
    def _():
        acc_scr[...] = jnp.zeros(acc_scr.shape, F32)

    for p in range(PEER_A_TILE // 2):
        prow = slice(p * pair, (p + 1) * pair)
        st = _dot_nt(u_ref[prow, :], xn_ref[...])
        for r in range(2):
            a = 2 * p + r
            gsum = None
            for h in range(PEER_HEADS):
                cnt = _bcast_row(cnt_ref, h, a)
                e1 = _bcast_row(e1_ref, h, a)
                picked = rank_ref[h] < cnt
                term = jnp.where(picked, e2_ref[h] * e1, jnp.zeros((), BF16))
                gsum = term if gsum is None else gsum + term
            act = _gelu_x2(st[r * PEER_N_KEYS:(r + 1) * PEER_N_KEYS, :].astype(BF16))
            w_scr[a * PEER_N_KEYS:(a + 1) * PEER_N_KEYS, :] = gsum * act
    acc_scr[...] += _dot(vt_ref[...], w_scr[...])

    @pl.when(k == pl.num_programs(1) - 1)
    def _():
        o_ref[...] = h_ref[...] + acc_scr[...].T


def _peer_main(h2, xn, u_bf, vt_bf, cnt, e1, rank, e2):
    T, D = h2.shape
    E = u_bf.shape[0]
    tm = PEER_TM
    te = PEER_A_TILE * PEER_N_KEYS
    tok = pl.BlockSpec((tm, D), lambda i, k: (i, 0))
    a_stat = pl.BlockSpec((PEER_HEADS, PEER_A_TILE, tm), lambda i, k: (0, k, i))
    b_stat = pl.BlockSpec((PEER_HEADS, PEER_N_KEYS, tm), lambda i, k: (0, 0, i))
    return pl.pallas_call(
        _peer_main_kernel,
        grid=(T // tm, E // te),
        in_specs=[tok,
                  pl.BlockSpec((te, D), lambda i, k: (k, 0)),
                  pl.BlockSpec((D, te), lambda i, k: (0, k)),
                  a_stat, a_stat, b_stat, b_stat, tok],
        out_specs=tok,
        out_shape=jax.ShapeDtypeStruct((T, D), F32),
        scratch_shapes=[pltpu.VMEM((D, tm), F32), pltpu.VMEM((te, tm), BF16)],
        compiler_params=_params("parallel", "arbitrary"),
        name="peer_main",
    )(xn, u_bf, vt_bf, cnt, e1, rank, e2, h2)


def _peer_layer(h, g, w_q, subkeys, u_tab, v_tab):
    B, S, D = h.shape
    h2 = h.reshape(B * S, D)
    wqh = w_q.astype(BF16)
    wql = (w_q - wqh.astype(F32)).astype(BF16)
    sk = subkeys.reshape(2 * PEER_HEADS, PEER_N_KEYS, PEER_D_HALF)
    skh = sk.astype(BF16)
    skl = (sk - skh.astype(F32)).astype(BF16)
    xn, cnt, e1, rank, e2 = _peer_route(h2, g, wqh, wql, skh, skl)
    out = _peer_main(h2, xn, u_tab.astype(BF16), v_tab.T.astype(BF16), cnt, e1, rank, e2)
    return out.reshape(B, S, D)


def _final_norm_kernel(h_ref, g_ref, o_ref):
    o_ref[...] = _rms(h_ref[...], g_ref[...])


def _final_norm(h2, g):
    T, D = h2.shape
    tm = 512
    return pl.pallas_call(
        _final_norm_kernel,
        grid=(T // tm,),
        in_specs=[pl.BlockSpec((tm, D), lambda i: (i, 0)), pl.BlockSpec((1, D), lambda i: (0, 0))],
        out_specs=pl.BlockSpec((tm, D), lambda i: (i, 0)),
        out_shape=jax.ShapeDtypeStruct((T, D), F32),
        compiler_params=_params("parallel"),
        name="final_norm",
    )(h2, g[None, :])


def kernel(x, positions, norm_mix, norm_ffn, norm_final, attn_w_in, mla_q_norm, mla_w_uq, mla_kv_norm,
           mla_w_ukv, attn_w_out, conv_w_in, conv_b_in, conv_w_dw, conv_b_dw, conv_ln_g, conv_ln_b,
           conv_w_out, conv_b_out, peer_w_q, peer_subkeys, peer_u, peer_v):
    B, S, D = x.shape
    assert D == D_MODEL and S % ATTN_TQ == 0 and S % MOBA_BLOCK == 0 and (B * S) % PEER_TM == 0
    depth = norm_mix.shape[0]
    tabs = _rope_tables(positions)
    h = x
    for layer in range(depth):
        i = layer // 2
        if layer % 2 == 0:
            w = _attn_weights(attn_w_in[i], mla_q_norm[i], mla_w_uq[i], mla_kv_norm[i], mla_w_ukv[i],
                              attn_w_out[i])
            h = _attention_layer(h, norm_mix[layer], w, tabs)
        else:
            h = _conv_layer(h, norm_mix[layer], conv_w_in[i], conv_b_in[i], conv_w_dw[i], conv_b_dw[i],
                            conv_ln_g[i], conv_ln_b[i], conv_w_out[i], conv_b_out[i])
        h = _peer_layer(h, norm_ffn[layer], peer_w_q[layer], peer_subkeys[layer], peer_u[layer],
                        peer_v[layer])
    return _final_norm(h.reshape(B * S, D), norm_final).reshape(B, S, D)
```
